```python
import jax, jax.numpy as jnp
from jax import lax
import numpy as np

D_MODEL = 2048
BATCH = 2
SEQ = 8192
DEPTH = 1
DEC_BATCH = 32
DEC_SEQ = 4
PAST_LEN = 16384
PAGE_SIZE = 128

N_HEADS_A = 8
HEAD_DIM_A = 128
D_A = N_HEADS_A * HEAD_DIM_A
MOBA_BLOCK = 256
MOBA_TOPK = 3
Q_BLOCK = 128
ROPE_THETA = 10000.0
N_HEADS_B = 8
KEY_DIM_B = 128
VAL_DIM_B = 128
D_BK = N_HEADS_B * KEY_DIM_B
D_BV = N_HEADS_B * VAL_DIM_B
HGRN_CHUNK = 64
D_FF = 5632
CONV_W = 3
PLE_DIM = 256
SPLITS = (D_A, D_A, D_A, D_BK, D_BK, D_BV, D_BV, D_MODEL, D_MODEL)
D_IN = 3 * D_A + 2 * D_BK + 2 * D_BV + 2 * D_MODEL
EPS = 1e-6
NEG_INF = -1e30
F32 = jnp.float32

kernel_name = "moba_hgrn2_hybrid_decode_step"


def rmsnorm(x, gain):
    xf = x.astype(F32)
    y = xf * lax.rsqrt(jnp.mean(xf * xf, axis=-1, keepdims=True) + EPS)
    return (y * gain.astype(F32)).astype(x.dtype)


def rope(x, pos):
    half = HEAD_DIM_A // 2
    inv_freq = jnp.power(ROPE_THETA, -jnp.arange(half, dtype=F32) * (2.0 / HEAD_DIM_A))
    ang = pos.astype(F32)[:, None] * inv_freq[None, :]
    cos = jnp.cos(ang)[:, None, :]
    sin = jnp.sin(ang)[:, None, :]
    xf = x.astype(F32)
    x1, x2 = xf[..., :half], xf[..., half:]
    return jnp.concatenate([x1 * cos - x2 * sin, x2 * cos + x1 * sin], axis=-1).astype(x.dtype)


def split_cols(z):
    outs, off = [], 0
    for w in SPLITS:
        outs.append(z[..., off:off + w])
        off += w
    return outs


def qk_heads(q, k, v, q_gain, k_gain, pos):
    B, T = q.shape[0], q.shape[1]
    shp = (B, T, N_HEADS_A, HEAD_DIM_A)
    q = rope(rmsnorm(q.reshape(shp), q_gain), pos)
    k = rope(rmsnorm(k.reshape(shp), k_gain), pos)
    return q, k, v.reshape(shp)


def moba_prompt(q, k, v):
    B, S = q.shape[0], q.shape[1]
    nb = -(-S // MOBA_BLOCK)
    pad = nb * MOBA_BLOCK - S
    padw = ((0, 0), (0, pad), (0, 0), (0, 0))
    k_blk = jnp.pad(k, padw).reshape(B, nb, MOBA_BLOCK, N_HEADS_A, HEAD_DIM_A)
    v_blk = jnp.pad(v, padw).reshape(B, nb, MOBA_BLOCK, N_HEADS_A, HEAD_DIM_A)
    k_mean = jnp.mean(k_blk.astype(F32), axis=2)
    k_top = min(MOBA_TOPK, nb)
    n_sel = k_top * MOBA_BLOCK
    scale = HEAD_DIM_A ** -0.5
    b_ix = jnp.arange(B)[:, None, None, None]
    h_ix = jnp.arange(N_HEADS_A)[None, None, :, None]
    blk_ids = jnp.arange(nb)
    q_off = jnp.arange(Q_BLOCK)
    k_off = jnp.arange(MOBA_BLOCK)

    def one_block(qb):
        q0 = qb * Q_BLOCK
        cur = q0 // MOBA_BLOCK
        qs = lax.dynamic_slice_in_dim(q, q0, Q_BLOCK, axis=1).astype(F32)
        gate = jnp.einsum("bqhd,bnhd->bqhn", qs, k_mean)
        gate = jnp.where(blk_ids < cur, gate, NEG_INF)
        _, sel = lax.top_k(gate, k_top)
        k_sel = k_blk[b_ix, sel, :, h_ix, :]
        v_sel = v_blk[b_ix, sel, :, h_ix, :]
        s_sel = jnp.einsum("bqhd,bqhkjd->bqhkj", qs, k_sel) * scale
        s_sel = jnp.where((jnp.arange(k_top) < cur)[:, None], s_sel, NEG_INF)
        k_own = lax.dynamic_index_in_dim(k_blk, cur, axis=1, keepdims=False)
        v_own = lax.dynamic_index_in_dim(v_blk, cur, axis=1, keepdims=False)
        s_own = jnp.einsum("bqhd,bjhd->bqhj", qs, k_own) * scale
        causal = (cur * MOBA_BLOCK + k_off)[None, :] <= (q0 + q_off)[:, None]
        s_own = jnp.where(causal[None, :, None, :], s_own, NEG_INF)
        p = jax.nn.softmax(jnp.concatenate([s_sel.reshape(B, Q_BLOCK, N_HEADS_A, n_sel), s_own], axis=-1), axis=-1)
        p_sel = p[..., :n_sel].reshape(B, Q_BLOCK, N_HEADS_A, k_top, MOBA_BLOCK)
        o = jnp.einsum("bqhkj,bqhkjd->bqhd", p_sel, v_sel) + jnp.einsum("bqhj,bjhd->bqhd", p[..., n_sel:], v_own)
        return o.astype(q.dtype)

    o = lax.map(one_block, jnp.arange(S // Q_BLOCK))
    return jnp.moveaxis(o, 0, 1).reshape(B, S, D_A)


def moba_sample(q, k_new, v_new, k_pool, v_pool, page_table):
    T = q.shape[1]
    n_pages = page_table.shape[1]
    ppb = MOBA_BLOCK // PAGE_SIZE
    n_full = (n_pages * PAGE_SIZE) // MOBA_BLOCK
    n_own = n_pages - n_full * ppb
    rows_own = n_own * PAGE_SIZE
    k_top = min(MOBA_TOPK, n_full)
    n_sel = k_top * MOBA_BLOCK
    scale = HEAD_DIM_A ** -0.5
    jpos = jnp.arange(rows_own + T)
    own_ok = (jpos[None, :] < rows_own) | ((jpos[None, :] - rows_own) <= jnp.arange(T)[:, None])
    h_ix = jnp.arange(N_HEADS_A)[None, :, None, None]

    def one_seq(args):
        qs, kn, vn, pt = args
        qs = qs.astype(F32)
        own_pages = pt[n_full * ppb:]
        k_own = jnp.concatenate([k_pool[own_pages].reshape(rows_own, N_HEADS_A, HEAD_DIM_A), kn], axis=0)
        v_own = jnp.concatenate([v_pool[own_pages].reshape(rows_own, N_HEADS_A, HEAD_DIM_A), vn], axis=0)
        s_own = jnp.einsum("thd,jhd->thj", qs, k_own) * scale
        s_own = jnp.where(own_ok[:, None, :], s_own, NEG_INF)
        if k_top > 0:
            blk_pages = pt[:n_full * ppb].reshape(n_full, ppb)
            k_full = k_pool[blk_pages]
            k_mean = jnp.mean(k_full.astype(F32), axis=(1, 2))
            gate = jnp.einsum("thd,nhd->thn", qs, k_mean)
            _, sel = lax.top_k(gate, k_top)
            sel_pages = blk_pages[sel]
            k_sel = k_pool[sel_pages, :, h_ix, :].reshape(T, N_HEADS_A, n_sel, HEAD_DIM_A)
            v_sel = v_pool[sel_pages, :, h_ix, :].reshape(T, N_HEADS_A, n_sel, HEAD_DIM_A)
            s_sel = jnp.einsum("thd,thjd->thj", qs, k_sel) * scale
            p = jax.nn.softmax(jnp.concatenate([s_sel, s_own], axis=-1), axis=-1)
            o = jnp.einsum("thj,thjd->thd", p[..., :n_sel], v_sel) + jnp.einsum("thj,jhd->thd", p[..., n_sel:], v_own)
        else:
            p = jax.nn.softmax(s_own, axis=-1)
            o = jnp.einsum("thj,jhd->thd", p, v_own)
        return o.astype(kn.dtype)

    o = lax.map(one_seq, (q, k_new, v_new, page_table))
    return o.reshape(q.shape[0], T, D_A)


def hgrn_inputs(q, f, i, lb):
    B, T = q.shape[0], q.shape[1]
    shp = (B, T, N_HEADS_B, KEY_DIM_B)
    z = f.astype(F32).reshape(shp)
    lb = lb.reshape(N_HEADS_B, KEY_DIM_B)
    forget = lb + (1.0 - lb) * jax.nn.sigmoid(z)
    key = (1.0 - lb) * jax.nn.sigmoid(-z)
    query = jax.nn.silu(q.astype(F32)).reshape(shp)
    value = i.astype(F32).reshape(B, T, N_HEADS_B, VAL_DIM_B)
    return query, key, value, jnp.log(forget)


def hgrn2_chunk(S0, inp):
    q, k, v, g = inp
    C = q.shape[1]
    A = jnp.cumsum(g, axis=1)
    causal = jnp.tril(jnp.ones((C, C), dtype=bool))
    diff = A[:, :, None] - A[:, None, :]
    decay = jnp.exp(jnp.where(causal[None, :, :, None, None], diff, NEG_INF))
    scores = jnp.einsum("bthk,bshk,btshk->bhts", q, k, decay)
    o = jnp.einsum("bhts,bshv->bthv", scores, v) + jnp.einsum("bthk,bhkv->bthv", q * jnp.exp(A), S0)
    A_last = A[:, -1]
    k_dec = k * jnp.exp(A_last[:, None] - A)
    S_new = jnp.exp(A_last)[..., None] * S0 + jnp.einsum("bshk,bshv->bhkv", k_dec, v)
    return S_new, o


def hgrn_run(q, k, v, g, S0, chunk):
    B, T = q.shape[0], q.shape[1]
    nc = T // chunk

    def to_chunks(a):
        return jnp.swapaxes(a.reshape(B, nc, chunk, *a.shape[2:]), 0, 1)

    S_fin, o = lax.scan(hgrn2_chunk, S0, (to_chunks(q), to_chunks(k), to_chunks(v), to_chunks(g)))
    return S_fin, jnp.swapaxes(o, 0, 1).reshape(B, T, N_HEADS_B, VAL_DIM_B)


def hgrn_output(o, g, gain):
    B, T = o.shape[0], o.shape[1]
    o = rmsnorm(o, gain) * jax.nn.silu(g.astype(F32)).reshape(B, T, N_HEADS_B, VAL_DIM_B)
    return o.reshape(B, T, D_BV).astype(g.dtype)


def conv_ffn(h, buf, w_gate, w_up, conv_w, conv_b, w_down):
    T = h.shape[1]
    g = h @ w_gate
    u = h @ w_up
    ext = jnp.concatenate([buf.astype(g.dtype), g], axis=1)
    c = conv_b + sum(conv_w[j] * ext[:, j:j + T] for j in range(CONV_W))
    y = (jax.nn.silu(c) * u) @ w_down
    return y, ext[:, T:]


def decoder_layer(x, p, pos, S0, conv_buf, attend, lb, wts):
    (norm_mix, w_in, q_norm, k_norm, g_norm_b, w_branch_a, w_branch_b, w_out,
     norm_ffn, w_ffn_gate, w_ffn_up, conv_w, conv_b, w_ffn_down, norm_ple, w_ple_gate, w_ple_proj) = wts
    T = x.shape[1]
    h = rmsnorm(x, norm_mix)
    qa, ka, va, qb, fb, ib, gb, gate_a, gate_b = split_cols(h @ w_in)
    qa, ka, va = qk_heads(qa, ka, va, q_norm, k_norm, pos)
    o_a = attend(qa, ka, va).astype(x.dtype)
    q_b, k_b, v_b, logf = hgrn_inputs(qb, fb, ib, lb)
    chunk = HGRN_CHUNK if T % HGRN_CHUNK == 0 else T
    S_new, o_b = hgrn_run(q_b, k_b, v_b, logf, S0.astype(F32), chunk)
    o_b = hgrn_output(o_b, gb, g_norm_b)
    mix = jax.nn.sigmoid(gate_a) * (o_a @ w_branch_a) + jax.nn.sigmoid(gate_b) * (o_b @ w_branch_b)
    x = x + mix @ w_out
    y_ffn, conv_new = conv_ffn(rmsnorm(x, norm_ffn), conv_buf, w_ffn_gate, w_ffn_up, conv_w, conv_b, w_ffn_down)
    x = x + y_ffn
    x = x + jax.nn.sigmoid(rmsnorm(x, norm_ple) @ w_ple_gate) * (p @ w_ple_proj)
    return x, ka, va, S_new.astype(x.dtype), conv_new


def setup_inputs(seed: int = 0) -> dict:
    key = jax.random.key(seed)
    ks = jax.random.split(key, 32)
    n_pages = PAST_LEN // PAGE_SIZE
    n_used = DEC_BATCH * n_pages
    n_pool = n_used + max(1, n_used // 4)

    def nrm(k, shape, scale):
        return jax.random.normal(k, shape, F32) * scale

    def gain(k, shape):
        return 1.0 + 0.1 * jax.random.normal(k, shape, F32)

    page_table = jax.random.permutation(ks[6], n_pool)[:n_used].reshape(DEC_BATCH, n_pages).astype(jnp.int32)
    return {
        "x_prompt": nrm(ks[0], (BATCH, SEQ, D_MODEL), 1.0),
        "x_sample": nrm(ks[1], (DEC_BATCH, DEC_SEQ, D_MODEL), 1.0),
        "cache_k": nrm(ks[2], (DEPTH, n_pool, PAGE_SIZE, N_HEADS_A, HEAD_DIM_A), 1.0),
        "cache_v": nrm(ks[3], (DEPTH, n_pool, PAGE_SIZE, N_HEADS_A, HEAD_DIM_A), 1.0),
        "state_hgrn": nrm(ks[4], (DEPTH, DEC_BATCH, N_HEADS_B, KEY_DIM_B, VAL_DIM_B), 0.5),
        "state_conv": nrm(ks[5], (DEPTH, DEC_BATCH, CONV_W - 1, D_FF), 1.0),
        "page_table": page_table,
        "p_prompt": nrm(ks[7], (DEPTH, BATCH, SEQ, PLE_DIM), 1.0),
        "p_sample": nrm(ks[8], (DEPTH, DEC_BATCH, DEC_SEQ, PLE_DIM), 1.0),
        "norm_mix": gain(ks[9], (DEPTH, D_MODEL)),
        "w_in": nrm(ks[10], (DEPTH, D_MODEL, D_IN), D_MODEL ** -0.5),
        "q_norm": gain(ks[11], (DEPTH, HEAD_DIM_A)),
        "k_norm": gain(ks[12], (DEPTH, HEAD_DIM_A)),
        "lb_logits": nrm(ks[13], (DEPTH + 1, D_BK), 0.5),
        "g_norm_b": gain(ks[14], (DEPTH, VAL_DIM_B)),
        "w_branch_a": nrm(ks[15], (DEPTH, D_A, D_MODEL), D_A ** -0.5),
        "w_branch_b": nrm(ks[16], (DEPTH, D_BV, D_MODEL), D_BV ** -0.5),
        "w_out": nrm(ks[17], (DEPTH, D_MODEL, D_MODEL), D_MODEL ** -0.5),
        "norm_ffn": gain(ks[18], (DEPTH, D_MODEL)),
        "w_ffn_gate": nrm(ks[19], (DEPTH, D_MODEL, D_FF), D_MODEL ** -0.5),
        "w_ffn_up": nrm(ks[20], (DEPTH, D_MODEL, D_FF), D_MODEL ** -0.5),
        "conv_w": nrm(ks[21], (DEPTH, CONV_W, D_FF), CONV_W ** -0.5),
        "conv_b": nrm(ks[22], (DEPTH, D_FF), 0.02),
        "w_ffn_down": nrm(ks[23], (DEPTH, D_FF, D_MODEL), D_FF ** -0.5),
        "norm_ple": gain(ks[24], (DEPTH, D_MODEL)),
        "w_ple_gate": nrm(ks[25], (DEPTH, D_MODEL, D_MODEL), D_MODEL ** -0.5),
        "w_ple_proj": nrm(ks[26], (DEPTH, PLE_DIM, D_MODEL), PLE_DIM ** -0.5),
    }


def reference(x_prompt, x_sample, cache_k, cache_v, state_hgrn, state_conv, page_table, p_prompt, p_sample,
              norm_mix, w_in, q_norm, k_norm, lb_logits, g_norm_b, w_branch_a, w_branch_b, w_out,
              norm_ffn, w_ffn_gate, w_ffn_up, conv_w, conv_b, w_ffn_down, norm_ple, w_ple_gate, w_ple_proj):
    S = x_prompt.shape[1]
    T = x_sample.shape[1]
    past_len = page_table.shape[1] * PAGE_SIZE
    pos_p = jnp.arange(S)
    pos_s = past_len + jnp.arange(T)
    lb_all = jnp.cumsum(jax.nn.softmax(lb_logits.astype(F32), axis=0), axis=0)
    xp, xs = x_prompt, x_sample
    kp_l, vp_l, sp_l, cp_l = [], [], [], []
    ks_l, vs_l, ss_l, cs_l = [], [], [], []
    for i in range(DEPTH):
        wts = (norm_mix[i], w_in[i], q_norm[i], k_norm[i], g_norm_b[i], w_branch_a[i], w_branch_b[i], w_out[i],
               norm_ffn[i], w_ffn_gate[i], w_ffn_up[i], conv_w[i], conv_b[i], w_ffn_down[i],
               norm_ple[i], w_ple_gate[i], w_ple_proj[i])
        S0_p = jnp.zeros((xp.shape[0], N_HEADS_B, KEY_DIM_B, VAL_DIM_B), F32)
        buf_p = jnp.zeros((xp.shape[0], CONV_W - 1, D_FF), xp.dtype)
        xp, kp, vp, sp, cp = decoder_layer(xp, p_prompt[i], pos_p, S0_p, buf_p, moba_prompt, lb_all[i], wts)
        k_pool_i, v_pool_i = cache_k[i], cache_v[i]

        def attend_sample(q, k, v, k_pool_i=k_pool_i, v_pool_i=v_pool_i):
            return moba_sample(q, k, v, k_pool_i, v_pool_i, page_table)

        xs, kss, vss, sss, css = decoder_layer(xs, p_sample[i], pos_s, state_hgrn[i], state_conv[i],
                                               attend_sample, lb_all[i], wts)
        kp_l.append(kp); vp_l.append(vp); sp_l.append(sp); cp_l.append(cp)
        ks_l.append(kss); vs_l.append(vss); ss_l.append(sss.astype(state_hgrn.dtype)); cs_l.append(css)
    k_prompt = jnp.stack(kp_l, 0)
    v_prompt = jnp.stack(vp_l, 0)
    hgrn_prompt = jnp.stack(sp_l, 0)
    conv_prompt = jnp.stack(cp_l, 0)
    k_sample = jnp.stack(ks_l, 0)
    v_sample = jnp.stack(vs_l, 0)
    hgrn_sample = jnp.stack(ss_l, 0)
    conv_sample = jnp.stack(cs_l, 0)
    return (xp, xs, k_prompt, v_prompt, hgrn_prompt, conv_prompt, k_sample, v_sample, hgrn_sample, conv_sample)
```

```python
import functools

import jax
import jax.numpy as jnp
from jax import lax
from jax.experimental import pallas as pl
from jax.experimental.pallas import tpu as pltpu

F32 = jnp.float32
BF16 = jnp.bfloat16
EPS = 1e-6
NEG_INF = -1e30
ROPE_THETA = 10000.0

HEAD = 128
N_HEADS = 8
MOBA_BLOCK = 256
MOBA_TOPK = 3
CONV_W = 3

VMEM_LIMIT = 52 * 1024 * 1024


def _cparams(*sem):
    return pltpu.CompilerParams(dimension_semantics=sem, vmem_limit_bytes=VMEM_LIMIT)


def _dot(a, b):
    return jnp.dot(a, b, preferred_element_type=F32)


def _dot_nt(a, b):
    return lax.dot_general(a, b, (((1,), (1,)), ((), ())), preferred_element_type=F32)


def _dot_tn(a, b):
    return lax.dot_general(a, b, (((0,), (0,)), ((), ())), preferred_element_type=F32)


def _sigmoid(x):
    return 1.0 / (1.0 + jnp.exp(-x))


def _silu(x):
    return x * _sigmoid(x)


def _rmsnorm_kernel(x_ref, g_ref, o_ref):
    x = x_ref[...]
    ms = jnp.mean(x * x, axis=-1, keepdims=True)
    o_ref[...] = (x * lax.rsqrt(ms + EPS) * g_ref[...]).astype(o_ref.dtype)


def _rmsnorm(x, gain, tm):
    M, D = x.shape
    return pl.pallas_call(
        _rmsnorm_kernel,
        grid=(M // tm,),
        in_specs=[pl.BlockSpec((tm, D), lambda i: (i, 0)), pl.BlockSpec((1, D), lambda i: (0, 0))],
        out_specs=pl.BlockSpec((tm, D), lambda i: (i, 0)),
        out_shape=jax.ShapeDtypeStruct((M, D), BF16),
        compiler_params=_cparams("parallel"),
        name="rmsnorm",
    )(x, gain.reshape(1, D))


def _mm_kernel(*refs, n_extra, epilogue):
    a_ref, w_ref = refs[:2]
    extras = refs[2:2 + n_extra]
    outs = refs[2 + n_extra:]
    acc = _dot(a_ref[...], w_ref[...])
    res = epilogue(acc, *[e[...] for e in extras])
    for o, r in zip(outs, res):
        o[...] = r.reshape(o.shape).astype(o.dtype)


def _matmul(a, w, *, col0, n_cols, tm, tn, epilogue, extras=(), extra_specs=(), out_shapes, out_specs, name):
    M, K = a.shape
    cb = col0 // tn
    return pl.pallas_call(
        functools.partial(_mm_kernel, n_extra=len(extras), epilogue=epilogue),
        grid=(n_cols // tn, M // tm),
        in_specs=[pl.BlockSpec((tm, K), lambda j, i: (i, 0)),
                  pl.BlockSpec((K, tn), lambda j, i: (0, cb + j))] + list(extra_specs),
        out_specs=out_specs,
        out_shape=out_shapes,
        compiler_params=_cparams("parallel", "parallel"),
        name=name,
    )(a, w, *extras)


def _qk_norm_rope(acc, gain, cos, sin):
    outs = []
    for h in range(acc.shape[1] // HEAD):
        z = acc[:, h * HEAD:(h + 1) * HEAD]
        y = z * lax.rsqrt(jnp.mean(z * z, axis=-1, keepdims=True) + EPS) * gain
        outs.append(y * cos + pltpu.roll(y, HEAD // 2, 1) * sin)
    return jnp.concatenate(outs, axis=1)


def _q_epilogue(acc, gain, cos, sin):
    return [_qk_norm_rope(acc, gain, cos, sin)]


def _k_epilogue(acc, gain, cos, sin, *, with_mean):
    k = _qk_norm_rope(acc, gain, cos, sin)
    if not with_mean:
        return [k, k]
    nblk = k.shape[0] // MOBA_BLOCK
    means = [jnp.mean(k[n * MOBA_BLOCK:(n + 1) * MOBA_BLOCK], axis=0, keepdims=True) for n in range(nblk)]
    return [k, k, jnp.concatenate(means, axis=0)]


def _rope_tables(pos):
    half = HEAD // 2
    inv_freq = jnp.power(ROPE_THETA, -jnp.arange(half, dtype=F32) * (2.0 / HEAD))
    ang = pos.astype(F32)[:, None] * inv_freq[None, :]
    c, s = jnp.cos(ang), jnp.sin(ang)
    return jnp.concatenate([c, c], axis=1), jnp.concatenate([-s, s], axis=1)


def _in_proj(h, w_in, q_gain, k_gain, cos, sin, *, tm, rope_tiles, with_mean, q_dtype):
    M = h.shape[0]
    DA = N_HEADS * HEAD
    row_tile = lambda j, i: (i, j)
    rope_spec = pl.BlockSpec((tm, HEAD), lambda j, i: (i % rope_tiles, 0))
    gain_spec = pl.BlockSpec((1, HEAD), lambda j, i: (0, 0))
    tile = pl.BlockSpec((tm, DA), row_tile)
    (q,) = _matmul(h, w_in, col0=0, n_cols=DA, tm=tm, tn=DA, epilogue=_q_epilogue,
                   extras=(q_gain.reshape(1, HEAD), cos, sin), extra_specs=(gain_spec, rope_spec, rope_spec),
                   out_shapes=[jax.ShapeDtypeStruct((M, DA), q_dtype)], out_specs=[tile], name="proj_q")
    k_shapes = [jax.ShapeDtypeStruct((M, DA), F32), jax.ShapeDtypeStruct((M, DA), BF16)]
    k_specs = [tile, tile]
    if with_mean:
        nb = tm // MOBA_BLOCK
        k_shapes.append(jax.ShapeDtypeStruct((M // tm, nb, DA), F32))
        k_specs.append(pl.BlockSpec((1, nb, DA), lambda j, i: (i, 0, j)))
    k_out = _matmul(h, w_in, col0=DA, n_cols=DA, tm=tm, tn=DA,
                    epilogue=functools.partial(_k_epilogue, with_mean=with_mean),
                    extras=(k_gain.reshape(1, HEAD), cos, sin), extra_specs=(gain_spec, rope_spec, rope_spec),
                    out_shapes=k_shapes, out_specs=k_specs, name="proj_k")
    v, v_bf = _matmul(h, w_in, col0=2 * DA, n_cols=DA, tm=tm, tn=DA, epilogue=lambda acc: [acc, acc],
                      out_shapes=[jax.ShapeDtypeStruct((M, DA), F32), jax.ShapeDtypeStruct((M, DA), BF16)],
                      out_specs=[tile, tile], name="proj_v")
    (hg,) = _matmul(h, w_in, col0=3 * DA, n_cols=4 * DA, tm=tm, tn=DA, epilogue=lambda acc: [acc],
                    out_shapes=[jax.ShapeDtypeStruct((M, 4 * DA), F32)], out_specs=[tile], name="proj_hgrn")
    (gates,) = _matmul(h, w_in, col0=7 * DA, n_cols=4 * DA, tm=tm, tn=DA, epilogue=lambda acc: [_sigmoid(acc)],
                       out_shapes=[jax.ShapeDtypeStruct((M, 4 * DA), F32)], out_specs=[tile], name="proj_gates")
    return q, k_out, v, v_bf, hg, gates


def _moba_prompt_kernel(q_ref, k_ref, v_ref, km_ref, o_ref, *, scale):
    cur = pl.program_id(2)
    B = MOBA_BLOCK
    q = q_ref[0]
    nb = km_ref.shape[1]
    row = lax.broadcasted_iota(jnp.int32, (B, B), 0)
    col = lax.broadcasted_iota(jnp.int32, (B, B), 1)

    def block_scores(n):
        start = pl.multiple_of(n * B, B)
        s = _dot_nt(q, k_ref[0, pl.ds(start, B), :]) * scale
        return s, v_ref[0, pl.ds(start, B), :]

    s, vb = block_scores(cur)
    s = jnp.where(col <= row, s, NEG_INF)
    m = jnp.max(s, axis=-1, keepdims=True)
    p = jnp.exp(s - m)
    l = jnp.sum(p, axis=-1, keepdims=True)
    acc = _dot(p.astype(BF16), vb)

    km = km_ref[0]
    km_hi = km.astype(BF16)
    km_lo = (km - km_hi.astype(F32)).astype(BF16)
    gate = _dot_nt(q, km_hi) + _dot_nt(q, km_lo)
    blk = lax.broadcasted_iota(jnp.int32, (B, nb), 1)
    g = jnp.where(blk < cur, gate, NEG_INF)
    sel = jnp.zeros((B, nb), F32)
    for _ in range(MOBA_TOPK):
        mx = jnp.max(g, axis=-1, keepdims=True)
        idx = jnp.min(jnp.where(g == mx, blk, nb), axis=-1, keepdims=True)
        hit = blk == idx
        sel = jnp.where(hit, 1.0, sel)
        g = jnp.where(hit, -jnp.inf, g)
    sel = jnp.where(blk < cur, sel, 0.0)

    def body(n, carry):
        m, l, acc = carry
        chosen = jnp.sum(jnp.where(blk == n, sel, 0.0), axis=-1, keepdims=True) > 0.0
        s, vb = block_scores(n)
        s = jnp.where(chosen, s, NEG_INF)
        m_new = jnp.maximum(m, jnp.max(s, axis=-1, keepdims=True))
        a = jnp.exp(m - m_new)
        p = jnp.exp(s - m_new)
        l = a * l + jnp.sum(p, axis=-1, keepdims=True)
        acc = a * acc + _dot(p.astype(BF16), vb)
        return m_new, l, acc

    m, l, acc = lax.fori_loop(0, cur, body, (m, l, acc))
    o_ref[0] = (acc / l).astype(o_ref.dtype)


def _moba_prompt(q, k, v, kmean):
    Bn, S, _ = q.shape
    nb = S // MOBA_BLOCK
    return pl.pallas_call(
        functools.partial(_moba_prompt_kernel, scale=HEAD ** -0.5),
        grid=(Bn, N_HEADS, nb),
        in_specs=[pl.BlockSpec((1, MOBA_BLOCK, HEAD), lambda b, h, i: (b, i, h)),
                  pl.BlockSpec((1, S, HEAD), lambda b, h, i: (b, 0, h)),
                  pl.BlockSpec((1, S, HEAD), lambda b, h, i: (b, 0, h)),
                  pl.BlockSpec((1, nb, HEAD), lambda b, h, i: (b, 0, h))],
        out_specs=pl.BlockSpec((1, MOBA_BLOCK, HEAD), lambda b, h, i: (b, i, h)),
        out_shape=jax.ShapeDtypeStruct(q.shape, BF16),
        compiler_params=_cparams("parallel", "parallel", "arbitrary"),
        name="moba_prompt",
    )(q, k, v, kmean)


def _hgrn_kernel(q_ref, z_ref, v_ref, g_ref, lb_ref, gn_ref, s0_ref, o_ref, s_ref, st_scr, *, blk, valid):
    c = pl.program_id(2)
    C = q_ref.shape[1]

    @pl.when(c == 0)
    def _():
        st_scr[...] = s0_ref[0, 0].T

    lb = lb_ref[...]
    z = z_ref[0]
    logf = jnp.log(lb + (1.0 - lb) * _sigmoid(z))
    kk = (1.0 - lb) * _sigmoid(-z)
    rowi = lax.broadcasted_iota(jnp.int32, (C, 1), 0)
    if valid < C:
        logf = jnp.where(rowi < valid, logf, 0.0)
        kk = jnp.where(rowi < valid, kk, 0.0)
    q = _silu(q_ref[0])
    v = v_ref[0].astype(BF16)

    tri = lax.broadcasted_iota(jnp.int32, (C, C), 0) >= lax.broadcasted_iota(jnp.int32, (C, C), 1)
    tri_bf = jnp.where(tri, 1.0, 0.0).astype(BF16)
    g1 = logf.astype(BF16)
    r1 = logf - g1.astype(F32)
    g2 = r1.astype(BF16)
    g3 = (r1 - g2.astype(F32)).astype(BF16)
    A = _dot(tri_bf, g1) + _dot(tri_bf, g2) + _dot(tri_bf, g3)
    a_last = A[C - 1:C, :]

    st = st_scr[...]
    o_inter = _dot_nt((q * jnp.exp(A)).astype(BF16), st.astype(BF16))
    k_dec = (kk * jnp.exp(a_last - A)).astype(BF16)
    st_new = st * jnp.exp(a_last) + _dot_tn(v, k_dec)
    st_scr[...] = st_new
    s_ref[0, 0] = st_new.T

    pieces = []
    for j in range(C // blk):
        lo, hi = j * blk, (j + 1) * blk
        a_ref_row = A[lo + blk // 2 - 1:lo + blk // 2, :]
        qt = (q[lo:hi] * jnp.exp(A[lo:hi] - a_ref_row)).astype(BF16)
        kt = (kk * jnp.exp(jnp.where(rowi < hi, a_ref_row - A, NEG_INF))).astype(BF16)
        pieces.append(_dot_nt(qt, kt))
    scores = pieces[0] if len(pieces) == 1 else jnp.concatenate(pieces, axis=0)
    scores = jnp.where(tri, scores, 0.0).astype(BF16)
    o = _dot(scores, v) + o_inter

    y = o * lax.rsqrt(jnp.mean(o * o, axis=-1, keepdims=True) + EPS) * gn_ref[...]
    o_ref[0] = (y * _silu(g_ref[0])).astype(o_ref.dtype)


def _hgrn(hg, lb, g_norm, s0, *, chunk, blk, valid):
    Bn, T, _ = hg.shape
    H = N_HEADS
    col = lambda g: pl.BlockSpec((1, chunk, HEAD), lambda b, h, c: (b, c, g * H + h))
    vec = pl.BlockSpec((1, HEAD), lambda b, h, c: (0, h))
    state = pl.BlockSpec((1, 1, HEAD, HEAD), lambda b, h, c: (b, h, 0, 0))
    return pl.pallas_call(
        functools.partial(_hgrn_kernel, blk=blk, valid=valid),
        grid=(Bn, H, T // chunk),
        in_specs=[col(0), col(1), col(2), col(3), vec, pl.BlockSpec((1, HEAD), lambda b, h, c: (0, 0)), state],
        out_specs=[pl.BlockSpec((1, chunk, HEAD), lambda b, h, c: (b, c, h)), state],
        out_shape=[jax.ShapeDtypeStruct((Bn, T, H * HEAD), BF16), jax.ShapeDtypeStruct(s0.shape, F32)],
        scratch_shapes=[pltpu.VMEM((HEAD, HEAD), F32)],
        compiler_params=_cparams("parallel", "parallel", "arbitrary"),
        name="hgrn",
    )(hg, hg, hg, hg, lb.reshape(1, H * HEAD), g_norm.reshape(1, HEAD), s0)


def _out_proj_epilogue(acc, x, gain):
    x1 = x + acc
    h2 = x1 * lax.rsqrt(jnp.mean(x1 * x1, axis=-1, keepdims=True) + EPS) * gain
    return [x1, h2]


def _merge_out(o_a, o_b, gates, x, w_ba, w_bb, w_out, norm_ffn, *, tm):
    M, D = x.shape
    tn = 1024
    tile = pl.BlockSpec((tm, tn), lambda j, i: (i, j))
    (ma,) = _matmul(o_a, w_ba, col0=0, n_cols=D, tm=tm, tn=tn, epilogue=lambda acc, ga: [ga * acc],
                    extras=(gates,), extra_specs=(tile,),
                    out_shapes=[jax.ShapeDtypeStruct((M, D), F32)], out_specs=[tile], name="branch_a")
    (mix,) = _matmul(o_b, w_bb, col0=0, n_cols=D, tm=tm, tn=tn, epilogue=lambda acc, gb, ma: [gb * acc + ma],
                     extras=(gates, ma),
                     extra_specs=(pl.BlockSpec((tm, tn), lambda j, i: (i, D // tn + j)), tile),
                     out_shapes=[jax.ShapeDtypeStruct((M, D), BF16)], out_specs=[tile], name="branch_b")
    full = pl.BlockSpec((tm, D), lambda j, i: (i, 0))
    return _matmul(mix, w_out, col0=0, n_cols=D, tm=tm, tn=D, epilogue=_out_proj_epilogue,
                   extras=(x, norm_ffn.reshape(1, D)),
                   extra_specs=(full, pl.BlockSpec((1, D), lambda j, i: (0, 0))),
                   out_shapes=[jax.ShapeDtypeStruct((M, D), F32), jax.ShapeDtypeStruct((M, D), BF16)],
                   out_specs=[full, full], name="out_proj")


def _ffn_kernel(h_ref, wg_ref, wu_ref, wd_ref, cw_ref, cb_ref, x_ref, e1_ref, e2_ref,
                y_ref, g_ref, acc_scr, tail_scr, *, tiles_per_seq, period):
    i, f = pl.program_id(0), pl.program_id(1)
    nf = pl.num_programs(1)
    tm = h_ref.shape[0]
    h = h_ref[...]
    g = _dot(h, wg_ref[...])
    u = _dot(h, wu_ref[...])
    rowi = lax.broadcasted_iota(jnp.int32, (tm, 1), 0)
    r1 = pltpu.roll(g, 1, 0)
    r2 = pltpu.roll(g, 2, 0)
    if period is None:
        @pl.when(i % tiles_per_seq == 0)
        def _():
            tail_scr[f, 0:1, :] = e2_ref[0]
            tail_scr[f, 1:2, :] = e1_ref[0]
        p2 = tail_scr[f, 0:1, :]
        p1 = tail_scr[f, 1:2, :]
        g1 = jnp.where(rowi == 0, p1, r1)
        g2 = jnp.where(rowi == 0, p2, jnp.where(rowi == 1, p1, r2))
        tail_scr[f, 0:2, :] = g[tm - 2:tm]
    else:
        t = rowi % period
        g1 = jnp.where(t == 0, e1_ref[...], r1)
        g2 = jnp.where(t < 2, e2_ref[...], r2)
    g_ref[...] = g.reshape(g_ref.shape) if period is not None else g[tm - 8:tm].reshape(g_ref.shape)
    cw = cw_ref[...]
    c = cb_ref[...] + cw[0:1] * g2 + cw[1:2] * g1 + cw[2:3] * g
    y = _dot((_silu(c) * u).astype(BF16), wd_ref[...])

    @pl.when(f == 0)
    def _():
        acc_scr[...] = y

    @pl.when(f > 0)
    def _():
        acc_scr[...] += y

    @pl.when(f == nf - 1)
    def _():
        y_ref[...] = x_ref[...] + acc_scr[...]


def _ffn(h2, x1, w_gate, w_up, w_down, conv_w, conv_b, e1, e2, *, tm, tf, rows_per_seq, period):
    M, D = x1.shape
    F = w_gate.shape[1]
    nf = F // tf
    if period is None:
        tps = rows_per_seq // tm
        e_spec = pl.BlockSpec((1, 1, tf), lambda i, f: (i // tps, 0, f))
        g_shape = jax.ShapeDtypeStruct((M // tm, 8, F), F32)
        g_spec = pl.BlockSpec((1, 8, tf), lambda i, f: (i, 0, f))
    else:
        tps = 1
        e_spec = pl.BlockSpec((tm, tf), lambda i, f: (i, f))
        g_shape = jax.ShapeDtypeStruct((M, F), F32)
        g_spec = pl.BlockSpec((tm, tf), lambda i, f: (i, f))
    return pl.pallas_call(
        functools.partial(_ffn_kernel, tiles_per_seq=tps, period=period),
        grid=(M // tm, nf),
        in_specs=[pl.BlockSpec((tm, D), lambda i, f: (i, 0)),
                  pl.BlockSpec((D, tf), lambda i, f: (0, f)),
                  pl.BlockSpec((D, tf), lambda i, f: (0, f)),
                  pl.BlockSpec((tf, D), lambda i, f: (f, 0)),
                  pl.BlockSpec((CONV_W, tf), lambda i, f: (0, f)),
                  pl.BlockSpec((1, tf), lambda i, f: (0, f)),
                  pl.BlockSpec((tm, D), lambda i, f: (i, 0)),
                  e_spec, e_spec],
        out_specs=[pl.BlockSpec((tm, D), lambda i, f: (i, 0)), g_spec],
        out_shape=[jax.ShapeDtypeStruct((M, D), F32), g_shape],
        scratch_shapes=[pltpu.VMEM((tm, D), F32), pltpu.VMEM((nf, 8, tf), F32)],
        compiler_params=_cparams("arbitrary", "arbitrary"),
        name="ffn",
    )(h2, w_gate, w_up, w_down, conv_w, conv_b.reshape(1, F), x1, e1, e2)


def _ple_kernel(x_ref, gain_ref, wg_ref, p_ref, wp_ref, o_ref):
    x = x_ref[...]
    hn = (x * lax.rsqrt(jnp.mean(x * x, axis=-1, keepdims=True) + EPS) * gain_ref[...]).astype(BF16)
    gate = _sigmoid(_dot(hn, wg_ref[...]))
    o_ref[...] = x + gate * _dot(p_ref[...].astype(BF16), wp_ref[...])


def _ple(x2, p, norm_ple, w_gate, w_proj, *, tm):
    M, D = x2.shape
    P = p.shape[1]
    return pl.pallas_call(
        _ple_kernel,
        grid=(M // tm,),
        in_specs=[pl.BlockSpec((tm, D), lambda i: (i, 0)), pl.BlockSpec((1, D), lambda i: (0, 0)),
                  pl.BlockSpec((D, D), lambda i: (0, 0)), pl.BlockSpec((tm, P), lambda i: (i, 0)),
                  pl.BlockSpec((P, D), lambda i: (0, 0))],
        out_specs=pl.BlockSpec((tm, D), lambda i: (i, 0)),
        out_shape=jax.ShapeDtypeStruct((M, D), F32),
        compiler_params=_cparams("parallel"),
        name="ple",
    )(x2, norm_ple.reshape(1, D), w_gate, p, w_proj)


PAGES_PER_STEP = 8


def _kmean_kernel(pt_ref, *refs):
    pages, o_ref = refs[:-1], refs[-1]
    ppb = len(pages) // o_ref.shape[1]
    rows = pages[0].shape[1]
    for b in range(o_ref.shape[1]):
        s = jnp.sum(pages[b * ppb][0], axis=0)
        for p in range(1, ppb):
            s = s + jnp.sum(pages[b * ppb + p][0], axis=0)
        o_ref[0, b] = s * (1.0 / (ppb * rows))


def _pool_block_means(pool, page_table, ppb):
    Bd, n_pages = page_table.shape
    _, page, H, hd = pool.shape
    P = PAGES_PER_STEP
    specs = [pl.BlockSpec((1, page, H, hd), functools.partial(lambda s, g, pt, p: (pt[s, g * P + p], 0, 0, 0), p=p))
             for p in range(P)]
    return pl.pallas_call(
        _kmean_kernel,
        grid_spec=pltpu.PrefetchScalarGridSpec(
            num_scalar_prefetch=1, grid=(Bd, n_pages // P), in_specs=specs,
            out_specs=pl.BlockSpec((1, P // ppb, H, hd), lambda s, g, pt: (s, g, 0, 0))),
        out_shape=jax.ShapeDtypeStruct((Bd, n_pages // ppb, H, hd), F32),
        compiler_params=_cparams("parallel", "arbitrary"),
        name="pool_block_means",
    )(page_table, *([pool] * P))


def _decode_select_kernel(q_ref, km_ref, o_ref):
    nb = km_ref.shape[2]
    for h in range(N_HEADS):
        q = q_ref[0, :, h * HEAD:(h + 1) * HEAD].astype(BF16)
        km = km_ref[0, h]
        km_hi = km.astype(BF16)
        km_lo = (km - km_hi.astype(F32)).astype(BF16)
        g = _dot_nt(q, km_hi) + _dot_nt(q, km_lo)
        blk = lax.broadcasted_iota(jnp.int32, g.shape, 1)
        lane = lax.broadcasted_iota(jnp.int32, (g.shape[0], 128), 1)
        out = jnp.zeros((g.shape[0], 128), jnp.int32)
        for k in range(MOBA_TOPK):
            mx = jnp.max(g, axis=-1, keepdims=True)
            idx = jnp.min(jnp.where(g == mx, blk, nb), axis=-1, keepdims=True)
            out = jnp.where(lane == k, idx, out)
            g = jnp.where(blk == idx, -jnp.inf, g)
        o_ref[0, h] = out


def _decode_select(q_pad, kmean):
    Bd, R, _ = q_pad.shape
    nb = kmean.shape[2]
    return pl.pallas_call(
        _decode_select_kernel,
        grid=(Bd,),
        in_specs=[pl.BlockSpec((1, R, N_HEADS * HEAD), lambda s: (s, 0, 0)),
                  pl.BlockSpec((1, N_HEADS, nb, HEAD), lambda s: (s, 0, 0, 0))],
        out_specs=pl.BlockSpec((1, N_HEADS, R, 128), lambda s: (s, 0, 0, 0)),
        out_shape=jax.ShapeDtypeStruct((Bd, N_HEADS, R, 128), jnp.int32),
        compiler_params=_cparams("parallel"),
        name="decode_select",
    )(q_pad, kmean)


def _decode_attn_kernel(pg_ref, q_ref, kn_ref, vn_ref, *refs, n_tok, n_slab, scale):
    slabs, o_ref = refs[:-1], refs[-1]
    R = q_ref.shape[1]
    kn = kn_ref[0]
    vn = vn_ref[0]
    jrow = lax.broadcasted_iota(jnp.int32, (R, 1), 0)
    o_ref[...] = jnp.zeros(o_ref.shape, o_ref.dtype)
    for t in range(n_tok):
        ks = jnp.concatenate([slabs[(t * 2) * n_slab + p][0] for p in range(n_slab)], axis=0).astype(BF16)
        vs = jnp.concatenate([slabs[(t * 2 + 1) * n_slab + p][0] for p in range(n_slab)], axis=0).astype(BF16)
        qrow = q_ref[0, t:t + 1, :]
        q8 = jnp.broadcast_to(qrow, (R, HEAD)).astype(BF16)
        s_sel = _dot_nt(q8, ks)[0:1] * scale
        s_own = jnp.sum(kn * qrow, axis=-1, keepdims=True) * scale
        s_own = jnp.where(jrow <= t, s_own, NEG_INF)
        m = jnp.maximum(jnp.max(s_sel, axis=-1, keepdims=True), jnp.max(s_own, axis=0, keepdims=True))
        p_sel = jnp.exp(s_sel - m)
        p_own = jnp.exp(s_own - m)
        l = jnp.sum(p_sel, axis=-1, keepdims=True) + jnp.sum(p_own, axis=0, keepdims=True)
        o = _dot(jnp.broadcast_to(p_sel, (R, p_sel.shape[1])).astype(BF16), vs)[0:1]
        o = o + jnp.sum(p_own * vn, axis=0, keepdims=True)
        o_ref[0, t:t + 1, :] = o / l


def _decode_attn(q_pad, kn_pad, vn_pad, pool_k, pool_v, sel_pages, *, n_tok):
    Bd, R, _ = q_pad.shape
    page = pool_k.shape[1]
    n_slab = sel_pages.shape[0] // (Bd * N_HEADS * n_tok)

    def slab_map(s, h, pg, *, t, p):
        return (pg[((s * N_HEADS + h) * n_tok + t) * n_slab + p], 0, h)

    slab_specs, slab_args = [], []
    for t in range(n_tok):
        for pool in (pool_k, pool_v):
            for p in range(n_slab):
                slab_specs.append(pl.BlockSpec((1, page, HEAD), functools.partial(slab_map, t=t, p=p)))
                slab_args.append(pool)
    row = pl.BlockSpec((1, R, HEAD), lambda s, h, pg: (s, 0, h))
    return pl.pallas_call(
        functools.partial(_decode_attn_kernel, n_tok=n_tok, n_slab=n_slab, scale=HEAD ** -0.5),
        grid_spec=pltpu.PrefetchScalarGridSpec(
            num_scalar_prefetch=1, grid=(Bd, N_HEADS), in_specs=[row, row, row] + slab_specs, out_specs=row),
        out_shape=jax.ShapeDtypeStruct(q_pad.shape, F32),
        compiler_params=_cparams("parallel", "arbitrary"),
        name="decode_attn",
    )(sel_pages, q_pad, kn_pad, vn_pad, *slab_args)


def _pad_rows(a, rows):
    return jnp.pad(a, ((0, 0), (0, rows - a.shape[1]), (0, 0)))


def kernel(x_prompt, x_sample, cache_k, cache_v, state_hgrn, state_conv, page_table, p_prompt, p_sample,
           norm_mix, w_in, q_norm, k_norm, lb_logits, g_norm_b, w_branch_a, w_branch_b, w_out,
           norm_ffn, w_ffn_gate, w_ffn_up, conv_w, conv_b, w_ffn_down, norm_ple, w_ple_gate, w_ple_proj):
    Bp, S, D = x_prompt.shape
    Bd, T, _ = x_sample.shape
    depth, n_pool, page, H, hd = cache_k.shape
    n_pages = page_table.shape[1]
    F = w_ffn_gate.shape[-1]
    DA = H * hd
    ppb = MOBA_BLOCK // page
    assert depth == 1 and H == N_HEADS and hd == HEAD
    assert (n_pages * page) % MOBA_BLOCK == 0, "past tokens must fill whole attention blocks"
    assert n_pages // ppb >= MOBA_TOPK and S % MOBA_BLOCK == 0 and T <= 8

    bf = lambda w: w[0].astype(BF16)
    w_in_b, w_ba, w_bb, w_o = bf(w_in), bf(w_branch_a), bf(w_branch_b), bf(w_out)
    w_fg, w_fu, w_fd, w_pg, w_pp = bf(w_ffn_gate), bf(w_ffn_up), bf(w_ffn_down), bf(w_ple_gate), bf(w_ple_proj)
    lb = jnp.cumsum(jax.nn.softmax(lb_logits.astype(F32), axis=0), axis=0)[0]

    Mp = Bp * S
    tm = 512
    xp = x_prompt.reshape(Mp, D)
    cos_p, sin_p = _rope_tables(jnp.arange(S))
    h = _rmsnorm(xp, norm_mix[0], tm)
    q, (k, k_bf, kmean), v, v_bf, hg, gates = _in_proj(
        h, w_in_b, q_norm[0], k_norm[0], cos_p, sin_p, tm=tm, rope_tiles=S // tm, with_mean=True, q_dtype=BF16)
    o_a = _moba_prompt(q.reshape(Bp, S, DA), k_bf.reshape(Bp, S, DA), v_bf.reshape(Bp, S, DA),
                       kmean.reshape(Bp, S // MOBA_BLOCK, DA))
    o_b, hgrn_p = _hgrn(hg.reshape(Bp, S, 4 * DA), lb, g_norm_b[0], jnp.zeros((Bp, H, hd, hd), F32),
                        chunk=256, blk=32, valid=256)
    x1, h2 = _merge_out(o_a.reshape(Mp, DA), o_b.reshape(Mp, DA), gates, xp, w_ba, w_bb, w_o, norm_ffn[0], tm=256)
    zeros_e = jnp.zeros((Bp, 1, F), F32)
    x2, g_tail = _ffn(h2, x1, w_fg, w_fu, w_fd, conv_w[0], conv_b[0], zeros_e, zeros_e,
                      tm=512, tf=512, rows_per_seq=S, period=None)
    y_prompt = _ple(x2, p_prompt[0].reshape(Mp, -1), norm_ple[0], w_pg, w_pp, tm=256).reshape(Bp, S, D)
    k_prompt = k.reshape(1, Bp, S, H, hd)
    v_prompt = v.reshape(1, Bp, S, H, hd)
    tps = S // 512
    conv_prompt = g_tail[tps - 1::tps, 8 - (CONV_W - 1):, :][None]

    Ms = Bd * T
    R = 8
    xs = x_sample.reshape(Ms, D)
    cos_s, sin_s = _rope_tables(n_pages * page + jnp.arange(T))
    cos_s, sin_s = jnp.tile(cos_s, (Bd, 1)), jnp.tile(sin_s, (Bd, 1))
    hs = _rmsnorm(xs, norm_mix[0], Ms)
    qs, (ks, _), vs, _, hgs, gates_s = _in_proj(
        hs, w_in_b, q_norm[0], k_norm[0], cos_s, sin_s, tm=Ms, rope_tiles=1, with_mean=False, q_dtype=F32)
    pool_k = cache_k.reshape(n_pool, page, H, hd)
    kmean_s = _pool_block_means(pool_k, page_table, ppb)
    q_pad = _pad_rows(qs.reshape(Bd, T, DA), R)
    sel = _decode_select(q_pad, jnp.transpose(kmean_s, (0, 2, 1, 3)))[:, :, :T, :MOBA_TOPK]
    blk_pages = page_table.reshape(Bd, n_pages // ppb, ppb)
    sel_pages = blk_pages[jnp.arange(Bd)[:, None, None, None], sel]
    o_as = _decode_attn(q_pad, _pad_rows(ks.reshape(Bd, T, DA), R), _pad_rows(vs.reshape(Bd, T, DA), R),
                        cache_k.reshape(n_pool, page, DA), cache_v.reshape(n_pool, page, DA),
                        sel_pages.reshape(-1).astype(jnp.int32), n_tok=T)[:, :T]
    RH = 16
    o_bs, hgrn_s = _hgrn(_pad_rows(hgs.reshape(Bd, T, 4 * DA), RH), lb, g_norm_b[0], state_hgrn[0],
                         chunk=RH, blk=RH, valid=T)
    x1s, h2s = _merge_out(o_as.reshape(Ms, DA).astype(BF16), o_bs[:, :T].reshape(Ms, DA), gates_s, xs,
                          w_ba, w_bb, w_o, norm_ffn[0], tm=Ms)
    buf = state_conv[0]
    zrow = jnp.zeros((Bd, 1, F), F32)
    e1 = jnp.concatenate([buf[:, 1:2], jnp.tile(zrow, (1, T - 1, 1))], axis=1).reshape(Ms, F)
    e2 = jnp.concatenate([buf[:, 0:1], buf[:, 1:2], jnp.tile(zrow, (1, T - 2, 1))], axis=1).reshape(Ms, F)
    x2s, g_s = _ffn(h2s, x1s, w_fg, w_fu, w_fd, conv_w[0], conv_b[0], e1, e2,
                    tm=Ms, tf=512, rows_per_seq=T, period=T)
    y_sample = _ple(x2s, p_sample[0].reshape(Ms, -1), norm_ple[0], w_pg, w_pp, tm=Ms).reshape(Bd, T, D)
    k_sample = ks.reshape(1, Bd, T, H, hd)
    v_sample = vs.reshape(1, Bd, T, H, hd)
    conv_sample = g_s.reshape(Bd, T, F)[:, T - (CONV_W - 1):][None]

    return (y_prompt, y_sample, k_prompt, v_prompt, hgrn_p[None], conv_prompt,
            k_sample, v_sample, hgrn_s[None], conv_sample)
```

```python
import functools

import jax
import jax.numpy as jnp
from jax import lax
from jax.experimental import pallas as pl
from jax.experimental.pallas import tpu as pltpu

F32 = jnp.float32
BF16 = jnp.bfloat16
EPS = 1e-6
NEG_INF = -1e30
ROPE_THETA = 10000.0
LOG2E = 1.4426950408889634

HEAD = 128
N_HEADS = 8
MOBA_BLOCK = 256
MOBA_TOPK = 3
CONV_W = 3

VMEM_LIMIT = 52 * 1024 * 1024


def _cparams(*sem):
    return pltpu.CompilerParams(dimension_semantics=sem, vmem_limit_bytes=VMEM_LIMIT)


def _dot(a, b):
    return jnp.dot(a, b, preferred_element_type=F32)


def _dot_nt(a, b):
    return lax.dot_general(a, b, (((1,), (1,)), ((), ())), preferred_element_type=F32)


def _dot_tn(a, b):
    return lax.dot_general(a, b, (((0,), (0,)), ((), ())), preferred_element_type=F32)


def _sigmoid(x):
    return 1.0 / (1.0 + jnp.exp(-x))


def _silu(x):
    return x * _sigmoid(x)


def _rmsnorm_kernel(x_ref, g_ref, o_ref):
    x = x_ref[...]
    ms = jnp.mean(x * x, axis=-1, keepdims=True)
    o_ref[...] = (x * lax.rsqrt(ms + EPS) * g_ref[...]).astype(o_ref.dtype)


def _rmsnorm(x, gain, tm):
    M, D = x.shape
    return pl.pallas_call(
        _rmsnorm_kernel,
        grid=(M // tm,),
        in_specs=[pl.BlockSpec((tm, D), lambda i: (i, 0)), pl.BlockSpec((1, D), lambda i: (0, 0))],
        out_specs=pl.BlockSpec((tm, D), lambda i: (i, 0)),
        out_shape=jax.ShapeDtypeStruct((M, D), BF16),
        compiler_params=_cparams("parallel"),
        name="rmsnorm",
    )(x, gain.reshape(1, D))


def _mm_kernel(*refs, n_extra, epilogue):
    a_ref, w_ref = refs[:2]
    extras = refs[2:2 + n_extra]
    outs = refs[2 + n_extra:]
    acc = _dot(a_ref[...], w_ref[...])
    res = epilogue(acc, *[e[...] for e in extras])
    for o, r in zip(outs, res):
        o[...] = r.reshape(o.shape).astype(o.dtype)


def _matmul(a, w, *, col0, n_cols, tm, tn, epilogue, extras=(), extra_specs=(), out_shapes, out_specs, name):
    M, K = a.shape
    cb = col0 // tn
    return pl.pallas_call(
        functools.partial(_mm_kernel, n_extra=len(extras), epilogue=epilogue),
        grid=(n_cols // tn, M // tm),
        in_specs=[pl.BlockSpec((tm, K), lambda j, i: (i, 0)),
                  pl.BlockSpec((K, tn), lambda j, i: (0, cb + j))] + list(extra_specs),
        out_specs=out_specs,
        out_shape=out_shapes,
        compiler_params=_cparams("parallel", "parallel"),
        name=name,
    )(a, w, *extras)


def _qk_norm_rope(acc, gain, cos, sin):
    outs = []
    for h in range(acc.shape[1] // HEAD):
        z = acc[:, h * HEAD:(h + 1) * HEAD]
        y = z * lax.rsqrt(jnp.mean(z * z, axis=-1, keepdims=True) + EPS) * gain
        outs.append(y * cos + pltpu.roll(y, HEAD // 2, 1) * sin)
    return jnp.concatenate(outs, axis=1)


def _q_epilogue(acc, gain, cos, sin, *, q_scale):
    return [_qk_norm_rope(acc, gain, cos, sin) * q_scale]


def _k_epilogue(acc, gain, cos, sin, *, with_mean):
    k = _qk_norm_rope(acc, gain, cos, sin)
    if not with_mean:
        return [k, k]
    nblk = k.shape[0] // MOBA_BLOCK
    means = [jnp.mean(k[n * MOBA_BLOCK:(n + 1) * MOBA_BLOCK], axis=0, keepdims=True) for n in range(nblk)]
    return [k, k, jnp.concatenate(means, axis=0)]


def _rope_tables(pos):
    half = HEAD // 2
    inv_freq = jnp.power(ROPE_THETA, -jnp.arange(half, dtype=F32) * (2.0 / HEAD))
    ang = pos.astype(F32)[:, None] * inv_freq[None, :]
    c, s = jnp.cos(ang), jnp.sin(ang)
    return jnp.concatenate([c, c], axis=1), jnp.concatenate([-s, s], axis=1)


def _in_proj(h, w_in, q_gain, k_gain, cos, sin, *, tm, rope_tiles, with_mean, q_dtype, q_scale):
    M = h.shape[0]
    DA = N_HEADS * HEAD
    row_tile = lambda j, i: (i, j)
    rope_spec = pl.BlockSpec((tm, HEAD), lambda j, i: (i % rope_tiles, 0))
    gain_spec = pl.BlockSpec((1, HEAD), lambda j, i: (0, 0))
    tile = pl.BlockSpec((tm, DA), row_tile)
    (q,) = _matmul(h, w_in, col0=0, n_cols=DA, tm=tm, tn=DA,
                   epilogue=functools.partial(_q_epilogue, q_scale=q_scale),
                   extras=(q_gain.reshape(1, HEAD), cos, sin), extra_specs=(gain_spec, rope_spec, rope_spec),
                   out_shapes=[jax.ShapeDtypeStruct((M, DA), q_dtype)], out_specs=[tile], name="proj_q")
    k_shapes = [jax.ShapeDtypeStruct((M, DA), F32), jax.ShapeDtypeStruct((M, DA), BF16)]
    k_specs = [tile, tile]
    if with_mean:
        nb = tm // MOBA_BLOCK
        k_shapes.append(jax.ShapeDtypeStruct((M // tm, nb, DA), F32))
        k_specs.append(pl.BlockSpec((1, nb, DA), lambda j, i: (i, 0, j)))
    k_out = _matmul(h, w_in, col0=DA, n_cols=DA, tm=tm, tn=DA,
                    epilogue=functools.partial(_k_epilogue, with_mean=with_mean),
                    extras=(k_gain.reshape(1, HEAD), cos, sin), extra_specs=(gain_spec, rope_spec, rope_spec),
                    out_shapes=k_shapes, out_specs=k_specs, name="proj_k")
    v, v_bf = _matmul(h, w_in, col0=2 * DA, n_cols=DA, tm=tm, tn=DA, epilogue=lambda acc: [acc, acc],
                      out_shapes=[jax.ShapeDtypeStruct((M, DA), F32), jax.ShapeDtypeStruct((M, DA), BF16)],
                      out_specs=[tile, tile], name="proj_v")
    (hg,) = _matmul(h, w_in, col0=3 * DA, n_cols=4 * DA, tm=tm, tn=DA, epilogue=lambda acc: [acc],
                    out_shapes=[jax.ShapeDtypeStruct((M, 4 * DA), F32)], out_specs=[tile], name="proj_hgrn")
    (gates,) = _matmul(h, w_in, col0=7 * DA, n_cols=4 * DA, tm=tm, tn=DA, epilogue=lambda acc: [_sigmoid(acc)],
                       out_shapes=[jax.ShapeDtypeStruct((M, 4 * DA), F32)], out_specs=[tile], name="proj_gates")
    return q, k_out, v, v_bf, hg, gates


MOBA_STEP_BLOCKS = 2


def _moba_prompt_kernel(q_ref, k_ref, v_ref, km_ref, ind_ref, o_ref, s_a, s_b):
    cur = pl.program_id(2)
    B = MOBA_BLOCK
    q = q_ref[0]
    nb = km_ref.shape[1]
    lane = lax.broadcasted_iota(jnp.int32, (1, HEAD), 1)
    ones_col = jnp.where(lane == 0, 1.0, 0.0).astype(BF16)

    def with_ones(v):
        return jnp.concatenate([v, jnp.broadcast_to(ones_col, v.shape)], axis=1)

    own = pl.multiple_of(cur * B, B)
    s = _dot_nt(q, k_ref[0, pl.ds(own, B), :])
    row = lax.broadcasted_iota(jnp.int32, (B, B), 0)
    col = lax.broadcasted_iota(jnp.int32, (B, B), 1)
    s = jnp.where(col <= row, s, NEG_INF)
    m = jnp.max(s, axis=-1, keepdims=True)
    acc = _dot(jnp.exp2(s - m).astype(BF16), with_ones(v_ref[0, pl.ds(own, B), :]))

    km = km_ref[0]
    km_hi = km.astype(BF16)
    km_lo = (km - km_hi.astype(F32)).astype(BF16)
    g = _dot_nt(km_hi, q) + _dot_nt(km_lo, q)
    blk = lax.broadcasted_iota(jnp.int32, (nb, B), 0)
    g = jnp.where(blk < cur, g, NEG_INF)
    sel = jnp.zeros((nb, B), F32)
    for _ in range(MOBA_TOPK):
        mx = jnp.max(g, axis=0, keepdims=True)
        idx = jnp.min(jnp.where(g == mx, blk, nb), axis=0, keepdims=True)
        hit = blk == idx
        sel = jnp.where(hit, 1.0, sel)
        g = jnp.where(hit, -jnp.inf, g)
    sel = jnp.where(blk < cur, sel, 0.0).astype(BF16)
    eye = (lax.broadcasted_iota(jnp.int32, (nb, HEAD), 0) == lax.broadcasted_iota(jnp.int32, (nb, HEAD), 1))
    sel_rows = _dot_tn(sel, jnp.where(eye, 1.0, 0.0).astype(BF16))
    q_aug = jnp.concatenate([q, jnp.where(sel_rows > 0.5, 0.0, NEG_INF).astype(BF16)], axis=1)

    W = MOBA_STEP_BLOCKS * B
    last_step = nb // MOBA_STEP_BLOCKS - 1
    n_trips = (cur + 2 * MOBA_STEP_BLOCKS - 1) // (2 * MOBA_STEP_BLOCKS)

    def scores(n):
        start = pl.multiple_of(n * W, W)
        k_aug = jnp.concatenate([k_ref[0, pl.ds(start, W), :], ind_ref[pl.ds(start, W), :]], axis=1)
        return _dot_nt(q_aug, k_aug)

    def softmax_step(s_ref, n, m, acc):
        s = s_ref[...]
        m_new = jnp.maximum(m, jnp.max(s, axis=-1, keepdims=True))
        p = jnp.exp2(s - m_new).astype(BF16)
        start = pl.multiple_of(n * W, W)
        return m_new, jnp.exp2(m - m_new) * acc + _dot(p, with_ones(v_ref[0, pl.ds(start, W), :]))

    def body(j, carry):
        m, acc = carry
        s_b[...] = scores(2 * j + 1)
        m, acc = softmax_step(s_a, 2 * j, m, acc)
        s_a[...] = scores(jnp.minimum(2 * j + 2, last_step))
        return softmax_step(s_b, 2 * j + 1, m, acc)

    s_a[...] = scores(0)
    m, acc = lax.fori_loop(0, n_trips, body, (m, acc))
    o_ref[0] = (acc[:, :HEAD] / acc[:, HEAD:HEAD + 1]).astype(o_ref.dtype)


def _moba_prompt(q, k, v, kmean):
    Bn, S, _ = q.shape
    nb = S // MOBA_BLOCK
    assert nb % (2 * MOBA_STEP_BLOCKS) == 0 and nb <= HEAD
    s_buf = pltpu.VMEM((MOBA_BLOCK, MOBA_STEP_BLOCKS * MOBA_BLOCK), F32)
    ind = (jnp.arange(S)[:, None] // MOBA_BLOCK == jnp.arange(HEAD)[None, :]).astype(BF16)
    return pl.pallas_call(
        _moba_prompt_kernel,
        grid=(Bn, N_HEADS, nb),
        in_specs=[pl.BlockSpec((1, MOBA_BLOCK, HEAD), lambda b, h, i: (b, i, h)),
                  pl.BlockSpec((1, S, HEAD), lambda b, h, i: (b, 0, h)),
                  pl.BlockSpec((1, S, HEAD), lambda b, h, i: (b, 0, h)),
                  pl.BlockSpec((1, nb, HEAD), lambda b, h, i: (b, 0, h)),
                  pl.BlockSpec((S, HEAD), lambda b, h, i: (0, 0))],
        out_specs=pl.BlockSpec((1, MOBA_BLOCK, HEAD), lambda b, h, i: (b, i, h)),
        out_shape=jax.ShapeDtypeStruct(q.shape, BF16),
        scratch_shapes=[s_buf, s_buf],
        compiler_params=_cparams("parallel", "parallel", "arbitrary"),
        name="moba_prompt",
    )(q, k, v, kmean, ind)


def _hgrn_chunk(q_raw, z, v, gate, lb, gain, st, *, blk, valid):
    C = z.shape[0]
    logf = jnp.log(lb + (1.0 - lb) * _sigmoid(z))
    kk = (1.0 - lb) * _sigmoid(-z)
    rowi = lax.broadcasted_iota(jnp.int32, (C, 1), 0)
    if valid < C:
        logf = jnp.where(rowi < valid, logf, 0.0)
        kk = jnp.where(rowi < valid, kk, 0.0)
    q = _silu(q_raw)
    v = v.astype(BF16)

    tri = lax.broadcasted_iota(jnp.int32, (C, C), 0) >= lax.broadcasted_iota(jnp.int32, (C, C), 1)
    tri_bf = jnp.where(tri, 1.0, 0.0).astype(BF16)
    g1 = logf.astype(BF16)
    r1 = logf - g1.astype(F32)
    g2 = r1.astype(BF16)
    g3 = (r1 - g2.astype(F32)).astype(BF16)
    A = _dot(tri_bf, g1) + _dot(tri_bf, g2) + _dot(tri_bf, g3)
    a_last = A[C - 1:C, :]

    o_inter = _dot_nt((q * jnp.exp(A)).astype(BF16), st.astype(BF16))
    k_dec = (kk * jnp.exp(a_last - A)).astype(BF16)
    st_new = st * jnp.exp(a_last) + _dot_tn(v, k_dec)

    pieces = []
    for j in range(C // blk):
        lo, hi = j * blk, (j + 1) * blk
        a_ref_row = A[lo + blk // 2 - 1:lo + blk // 2, :]
        qt = (q[lo:hi] * jnp.exp(A[lo:hi] - a_ref_row)).astype(BF16)
        kt = (kk * jnp.exp(jnp.where(rowi < hi, a_ref_row - A, NEG_INF))).astype(BF16)
        pieces.append(_dot_nt(qt, kt))
    scores = pieces[0] if len(pieces) == 1 else jnp.concatenate(pieces, axis=0)
    scores = jnp.where(tri, scores, 0.0).astype(BF16)
    o = _dot(scores, v) + o_inter

    y = o * lax.rsqrt(jnp.mean(o * o, axis=-1, keepdims=True) + EPS) * gain
    return y * _silu(gate), st_new


def _hgrn_kernel(q_ref, z_ref, v_ref, g_ref, lb_ref, gn_ref, s0_ref, o_ref, s_ref, st_scr, *, blk):
    @pl.when(pl.program_id(2) == 0)
    def _():
        st_scr[...] = s0_ref[0, 0].T

    o, st_new = _hgrn_chunk(q_ref[0], z_ref[0], v_ref[0], g_ref[0], lb_ref[...], gn_ref[...], st_scr[...],
                            blk=blk, valid=q_ref.shape[1])
    st_scr[...] = st_new
    s_ref[0, 0] = st_new.T
    o_ref[0] = o.astype(o_ref.dtype)


def _hgrn_decode_kernel(hg_ref, lb_ref, gn_ref, s0_ref, o_ref, s_ref, *, valid):
    D = N_HEADS * HEAD
    for h in range(N_HEADS):
        cols = [hg_ref[0, :, g * D + h * HEAD:g * D + (h + 1) * HEAD] for g in range(4)]
        o, st_new = _hgrn_chunk(*cols, lb_ref[:, h * HEAD:(h + 1) * HEAD], gn_ref[...], s0_ref[0, h].T,
                                blk=hg_ref.shape[1], valid=valid)
        s_ref[0, h] = st_new.T
        o_ref[0, :, h * HEAD:(h + 1) * HEAD] = o.astype(o_ref.dtype)


def _hgrn_decode(hg, lb, g_norm, s0, *, valid):
    Bn, R, _ = hg.shape
    D = N_HEADS * HEAD
    state = pl.BlockSpec((1, N_HEADS, HEAD, HEAD), lambda b: (b, 0, 0, 0))
    return pl.pallas_call(
        functools.partial(_hgrn_decode_kernel, valid=valid),
        grid=(Bn,),
        in_specs=[pl.BlockSpec((1, R, 4 * D), lambda b: (b, 0, 0)), pl.BlockSpec((1, D), lambda b: (0, 0)),
                  pl.BlockSpec((1, HEAD), lambda b: (0, 0)), state],
        out_specs=[pl.BlockSpec((1, R, D), lambda b: (b, 0, 0)), state],
        out_shape=[jax.ShapeDtypeStruct((Bn, R, D), BF16), jax.ShapeDtypeStruct(s0.shape, F32)],
        compiler_params=_cparams("parallel"),
        name="hgrn_decode",
    )(hg, lb.reshape(1, D), g_norm.reshape(1, HEAD), s0)


def _hgrn(hg, lb, g_norm, s0, *, chunk, blk):
    Bn, T, _ = hg.shape
    H = N_HEADS
    col = lambda g: pl.BlockSpec((1, chunk, HEAD), lambda b, h, c: (b, c, g * H + h))
    vec = pl.BlockSpec((1, HEAD), lambda b, h, c: (0, h))
    state = pl.BlockSpec((1, 1, HEAD, HEAD), lambda b, h, c: (b, h, 0, 0))
    return pl.pallas_call(
        functools.partial(_hgrn_kernel, blk=blk),
        grid=(Bn, H, T // chunk),
        in_specs=[col(0), col(1), col(2), col(3), vec, pl.BlockSpec((1, HEAD), lambda b, h, c: (0, 0)), state],
        out_specs=[pl.BlockSpec((1, chunk, HEAD), lambda b, h, c: (b, c, h)), state],
        out_shape=[jax.ShapeDtypeStruct((Bn, T, H * HEAD), BF16), jax.ShapeDtypeStruct(s0.shape, F32)],
        scratch_shapes=[pltpu.VMEM((HEAD, HEAD), F32)],
        compiler_params=_cparams("parallel", "parallel", "arbitrary"),
        name="hgrn",
    )(hg, hg, hg, hg, lb.reshape(1, H * HEAD), g_norm.reshape(1, HEAD), s0)


def _out_proj_epilogue(acc, x, gain):
    x1 = x + acc
    h2 = x1 * lax.rsqrt(jnp.mean(x1 * x1, axis=-1, keepdims=True) + EPS) * gain
    return [x1, h2]


def _merge_out(o_a, o_b, gates, x, w_ba, w_bb, w_out, norm_ffn, *, tm):
    M, D = x.shape
    tn = 1024
    tile = pl.BlockSpec((tm, tn), lambda j, i: (i, j))
    (ma,) = _matmul(o_a, w_ba, col0=0, n_cols=D, tm=tm, tn=tn, epilogue=lambda acc, ga: [ga * acc],
                    extras=(gates,), extra_specs=(tile,),
                    out_shapes=[jax.ShapeDtypeStruct((M, D), F32)], out_specs=[tile], name="branch_a")
    (mix,) = _matmul(o_b, w_bb, col0=0, n_cols=D, tm=tm, tn=tn, epilogue=lambda acc, gb, ma: [gb * acc + ma],
                     extras=(gates, ma),
                     extra_specs=(pl.BlockSpec((tm, tn), lambda j, i: (i, D // tn + j)), tile),
                     out_shapes=[jax.ShapeDtypeStruct((M, D), BF16)], out_specs=[tile], name="branch_b")
    full = pl.BlockSpec((tm, D), lambda j, i: (i, 0))
    return _matmul(mix, w_out, col0=0, n_cols=D, tm=tm, tn=D, epilogue=_out_proj_epilogue,
                   extras=(x, norm_ffn.reshape(1, D)),
                   extra_specs=(full, pl.BlockSpec((1, D), lambda j, i: (0, 0))),
                   out_shapes=[jax.ShapeDtypeStruct((M, D), F32), jax.ShapeDtypeStruct((M, D), BF16)],
                   out_specs=[full, full], name="out_proj")


def _ffn_kernel(h_ref, wg_ref, wu_ref, wd_ref, cw_ref, cb_ref, x_ref, e1_ref, e2_ref,
                y_ref, g_ref, acc_scr, tail_scr, *, tiles_per_seq, period):
    i, f = pl.program_id(0), pl.program_id(1)
    nf = pl.num_programs(1)
    tm = h_ref.shape[0]
    h = h_ref[...]
    g = _dot(h, wg_ref[...])
    u = _dot(h, wu_ref[...])
    rowi = lax.broadcasted_iota(jnp.int32, (tm, 1), 0)
    r1 = pltpu.roll(g, 1, 0)
    r2 = pltpu.roll(g, 2, 0)
    if period is None:
        @pl.when(i % tiles_per_seq == 0)
        def _():
            tail_scr[f, 0:1, :] = e2_ref[0]
            tail_scr[f, 1:2, :] = e1_ref[0]
        p2 = tail_scr[f, 0:1, :]
        p1 = tail_scr[f, 1:2, :]
        g1 = jnp.where(rowi == 0, p1, r1)
        g2 = jnp.where(rowi == 0, p2, jnp.where(rowi == 1, p1, r2))
        tail_scr[f, 0:2, :] = g[tm - 2:tm]
    else:
        t = rowi % period
        g1 = jnp.where(t == 0, e1_ref[...], r1)
        g2 = jnp.where(t < 2, e2_ref[...], r2)
    g_ref[...] = g.reshape(g_ref.shape) if period is not None else g[tm - 8:tm].reshape(g_ref.shape)
    cw = cw_ref[...]
    c = cb_ref[...] + cw[0:1] * g2 + cw[1:2] * g1 + cw[2:3] * g
    y = _dot((_silu(c) * u).astype(BF16), wd_ref[...])

    @pl.when(f == 0)
    def _():
        acc_scr[...] = y

    @pl.when(f > 0)
    def _():
        acc_scr[...] += y

    @pl.when(f == nf - 1)
    def _():
        y_ref[...] = x_ref[...] + acc_scr[...]


def _ffn(h2, x1, w_gate, w_up, w_down, conv_w, conv_b, e1, e2, *, tm, tf, rows_per_seq, period):
    M, D = x1.shape
    F = w_gate.shape[1]
    nf = F // tf
    if period is None:
        tps = rows_per_seq // tm
        e_spec = pl.BlockSpec((1, 1, tf), lambda i, f: (i // tps, 0, f))
        g_shape = jax.ShapeDtypeStruct((M // tm, 8, F), F32)
        g_spec = pl.BlockSpec((1, 8, tf), lambda i, f: (i, 0, f))
    else:
        tps = 1
        e_spec = pl.BlockSpec((tm, tf), lambda i, f: (i, f))
        g_shape = jax.ShapeDtypeStruct((M, F), F32)
        g_spec = pl.BlockSpec((tm, tf), lambda i, f: (i, f))
    return pl.pallas_call(
        functools.partial(_ffn_kernel, tiles_per_seq=tps, period=period),
        grid=(M // tm, nf),
        in_specs=[pl.BlockSpec((tm, D), lambda i, f: (i, 0)),
                  pl.BlockSpec((D, tf), lambda i, f: (0, f)),
                  pl.BlockSpec((D, tf), lambda i, f: (0, f)),
                  pl.BlockSpec((tf, D), lambda i, f: (f, 0)),
                  pl.BlockSpec((CONV_W, tf), lambda i, f: (0, f)),
                  pl.BlockSpec((1, tf), lambda i, f: (0, f)),
                  pl.BlockSpec((tm, D), lambda i, f: (i, 0)),
                  e_spec, e_spec],
        out_specs=[pl.BlockSpec((tm, D), lambda i, f: (i, 0)), g_spec],
        out_shape=[jax.ShapeDtypeStruct((M, D), F32), g_shape],
        scratch_shapes=[pltpu.VMEM((tm, D), F32), pltpu.VMEM((nf, 8, tf), F32)],
        compiler_params=_cparams("arbitrary", "arbitrary"),
        name="ffn",
    )(h2, w_gate, w_up, w_down, conv_w, conv_b.reshape(1, F), x1, e1, e2)


def _ple_kernel(x_ref, gain_ref, wg_ref, p_ref, wp_ref, o_ref):
    x = x_ref[...]
    hn = (x * lax.rsqrt(jnp.mean(x * x, axis=-1, keepdims=True) + EPS) * gain_ref[...]).astype(BF16)
    gate = _sigmoid(_dot(hn, wg_ref[...]))
    o_ref[...] = x + gate * _dot(p_ref[...].astype(BF16), wp_ref[...])


def _ple(x2, p, norm_ple, w_gate, w_proj, *, tm):
    M, D = x2.shape
    P = p.shape[1]
    return pl.pallas_call(
        _ple_kernel,
        grid=(M // tm,),
        in_specs=[pl.BlockSpec((tm, D), lambda i: (i, 0)), pl.BlockSpec((1, D), lambda i: (0, 0)),
                  pl.BlockSpec((D, D), lambda i: (0, 0)), pl.BlockSpec((tm, P), lambda i: (i, 0)),
                  pl.BlockSpec((P, D), lambda i: (0, 0))],
        out_specs=pl.BlockSpec((tm, D), lambda i: (i, 0)),
        out_shape=jax.ShapeDtypeStruct((M, D), F32),
        compiler_params=_cparams("parallel"),
        name="ple",
    )(x2, norm_ple.reshape(1, D), w_gate, p, w_proj)


PAGES_PER_STEP = 8


def _kmean_kernel(pt_ref, *refs):
    pages, o_ref = refs[:-1], refs[-1]
    ppb = len(pages) // o_ref.shape[1]
    rows = pages[0].shape[1]
    for b in range(o_ref.shape[1]):
        s = jnp.sum(pages[b * ppb][0], axis=0)
        for p in range(1, ppb):
            s = s + jnp.sum(pages[b * ppb + p][0], axis=0)
        o_ref[0, b] = s * (1.0 / (ppb * rows))


def _pool_block_means(pool, page_table, ppb):
    Bd, n_pages = page_table.shape
    _, page, H, hd = pool.shape
    P = PAGES_PER_STEP
    specs = [pl.BlockSpec((1, page, H, hd), functools.partial(lambda s, g, pt, p: (pt[s, g * P + p], 0, 0, 0), p=p))
             for p in range(P)]
    return pl.pallas_call(
        _kmean_kernel,
        grid_spec=pltpu.PrefetchScalarGridSpec(
            num_scalar_prefetch=1, grid=(Bd, n_pages // P), in_specs=specs,
            out_specs=pl.BlockSpec((1, P // ppb, H, hd), lambda s, g, pt: (s, g, 0, 0))),
        out_shape=jax.ShapeDtypeStruct((Bd, n_pages // ppb, H, hd), F32),
        compiler_params=_cparams("parallel", "arbitrary"),
        name="pool_block_means",
    )(page_table, *([pool] * P))


def _decode_select_kernel(q_ref, km_ref, o_ref):
    nb = km_ref.shape[2]
    for h in range(N_HEADS):
        q = q_ref[0, :, h * HEAD:(h + 1) * HEAD].astype(BF16)
        km = km_ref[0, h]
        km_hi = km.astype(BF16)
        km_lo = (km - km_hi.astype(F32)).astype(BF16)
        g = _dot_nt(q, km_hi) + _dot_nt(q, km_lo)
        blk = lax.broadcasted_iota(jnp.int32, g.shape, 1)
        lane = lax.broadcasted_iota(jnp.int32, (g.shape[0], 128), 1)
        out = jnp.zeros((g.shape[0], 128), jnp.int32)
        for k in range(MOBA_TOPK):
            mx = jnp.max(g, axis=-1, keepdims=True)
            idx = jnp.min(jnp.where(g == mx, blk, nb), axis=-1, keepdims=True)
            out = jnp.where(lane == k, idx, out)
            g = jnp.where(blk == idx, -jnp.inf, g)
        o_ref[0, h] = out


def _decode_select(q_pad, kmean):
    Bd, R, _ = q_pad.shape
    nb = kmean.shape[2]
    return pl.pallas_call(
        _decode_select_kernel,
        grid=(Bd,),
        in_specs=[pl.BlockSpec((1, R, N_HEADS * HEAD), lambda s: (s, 0, 0)),
                  pl.BlockSpec((1, N_HEADS, nb, HEAD), lambda s: (s, 0, 0, 0))],
        out_specs=pl.BlockSpec((1, N_HEADS, R, 128), lambda s: (s, 0, 0, 0)),
        out_shape=jax.ShapeDtypeStruct((Bd, N_HEADS, R, 128), jnp.int32),
        compiler_params=_cparams("parallel"),
        name="decode_select",
    )(q_pad, kmean)


def _decode_attn_kernel(pg_ref, q_ref, kn_ref, vn_ref, pk_ref, pv_ref, o_ref, kbuf, vbuf, sem, *, n_tok, n_slab, scale):
    nh = pl.num_programs(1)
    step = pl.program_id(0) * nh + pl.program_id(1)
    n_steps = pl.num_programs(0) * nh
    slot = step % 2
    per_step = n_tok * n_slab

    def slab_copies(st, sl):
        head = st % nh
        copies = []
        for j in range(per_step):
            pg = pg_ref[st * per_step + j]
            copies.append(pltpu.make_async_copy(pk_ref.at[pg, :, head, :], kbuf.at[sl, j], sem.at[0, sl]))
            copies.append(pltpu.make_async_copy(pv_ref.at[pg, :, head, :], vbuf.at[sl, j], sem.at[1, sl]))
        return copies

    @pl.when(step == 0)
    def _():
        for c in slab_copies(step, slot):
            c.start()

    @pl.when(step + 1 < n_steps)
    def _():
        for c in slab_copies(step + 1, 1 - slot):
            c.start()

    for c in slab_copies(step, slot):
        c.wait()

    R = q_ref.shape[1]
    page = kbuf.shape[2]
    kn = kn_ref[0]
    vn = vn_ref[0]
    jrow = lax.broadcasted_iota(jnp.int32, (R, 1), 0)
    o_ref[...] = jnp.zeros(o_ref.shape, o_ref.dtype)
    for t in range(n_tok):
        ks = kbuf[slot, t * n_slab:(t + 1) * n_slab].reshape(n_slab * page, HEAD).astype(BF16)
        vs = vbuf[slot, t * n_slab:(t + 1) * n_slab].reshape(n_slab * page, HEAD).astype(BF16)
        qrow = q_ref[0, t:t + 1, :]
        q8 = jnp.broadcast_to(qrow, (R, HEAD)).astype(BF16)
        s_sel = _dot_nt(q8, ks)[0:1] * scale
        s_own = jnp.sum(kn * qrow, axis=-1, keepdims=True) * scale
        s_own = jnp.where(jrow <= t, s_own, NEG_INF)
        m = jnp.maximum(jnp.max(s_sel, axis=-1, keepdims=True), jnp.max(s_own, axis=0, keepdims=True))
        p_sel = jnp.exp(s_sel - m)
        p_own = jnp.exp(s_own - m)
        l = jnp.sum(p_sel, axis=-1, keepdims=True) + jnp.sum(p_own, axis=0, keepdims=True)
        o = _dot(jnp.broadcast_to(p_sel, (R, p_sel.shape[1])).astype(BF16), vs)[0:1]
        o = o + jnp.sum(p_own * vn, axis=0, keepdims=True)
        o_ref[0, t:t + 1, :] = o / l


def _decode_attn(q_pad, kn_pad, vn_pad, pool_k, pool_v, sel_pages, *, n_tok):
    Bd, R, _ = q_pad.shape
    page = pool_k.shape[1]
    n_slab = sel_pages.shape[0] // (Bd * N_HEADS * n_tok)
    row = pl.BlockSpec((1, R, HEAD), lambda s, h, pg: (s, 0, h))
    hbm = pl.BlockSpec(memory_space=pl.ANY)
    slabs = pltpu.VMEM((2, n_tok * n_slab, page, HEAD), F32)
    return pl.pallas_call(
        functools.partial(_decode_attn_kernel, n_tok=n_tok, n_slab=n_slab, scale=HEAD ** -0.5),
        grid_spec=pltpu.PrefetchScalarGridSpec(
            num_scalar_prefetch=1, grid=(Bd, N_HEADS), in_specs=[row, row, row, hbm, hbm], out_specs=row,
            scratch_shapes=[slabs, slabs, pltpu.SemaphoreType.DMA((2, 2))]),
        out_shape=jax.ShapeDtypeStruct(q_pad.shape, F32),
        compiler_params=_cparams("arbitrary", "arbitrary"),
        name="decode_attn",
    )(sel_pages, q_pad, kn_pad, vn_pad, pool_k, pool_v)


def _pad_rows(a, rows):
    return jnp.pad(a, ((0, 0), (0, rows - a.shape[1]), (0, 0)))


def kernel(x_prompt, x_sample, cache_k, cache_v, state_hgrn, state_conv, page_table, p_prompt, p_sample,
           norm_mix, w_in, q_norm, k_norm, lb_logits, g_norm_b, w_branch_a, w_branch_b, w_out,
           norm_ffn, w_ffn_gate, w_ffn_up, conv_w, conv_b, w_ffn_down, norm_ple, w_ple_gate, w_ple_proj):
    Bp, S, D = x_prompt.shape
    Bd, T, _ = x_sample.shape
    depth, n_pool, page, H, hd = cache_k.shape
    n_pages = page_table.shape[1]
    F = w_ffn_gate.shape[-1]
    DA = H * hd
    ppb = MOBA_BLOCK // page
    assert depth == 1 and H == N_HEADS and hd == HEAD
    assert (n_pages * page) % MOBA_BLOCK == 0, "past tokens must fill whole attention blocks"
    assert n_pages // ppb >= MOBA_TOPK and S % MOBA_BLOCK == 0 and T <= 8

    bf = lambda w: w[0].astype(BF16)
    w_in_b, w_ba, w_bb, w_o = bf(w_in), bf(w_branch_a), bf(w_branch_b), bf(w_out)
    w_fg, w_fu, w_fd, w_pg, w_pp = bf(w_ffn_gate), bf(w_ffn_up), bf(w_ffn_down), bf(w_ple_gate), bf(w_ple_proj)
    lb = jnp.cumsum(jax.nn.softmax(lb_logits.astype(F32), axis=0), axis=0)[0]

    Mp = Bp * S
    tm = 512
    xp = x_prompt.reshape(Mp, D)
    cos_p, sin_p = _rope_tables(jnp.arange(S))
    h = _rmsnorm(xp, norm_mix[0], tm)
    q, (k, k_bf, kmean), v, v_bf, hg, gates = _in_proj(
        h, w_in_b, q_norm[0], k_norm[0], cos_p, sin_p, tm=tm, rope_tiles=S // tm, with_mean=True, q_dtype=BF16,
        q_scale=HEAD ** -0.5 * LOG2E)
    o_a = _moba_prompt(q.reshape(Bp, S, DA), k_bf.reshape(Bp, S, DA), v_bf.reshape(Bp, S, DA),
                       kmean.reshape(Bp, S // MOBA_BLOCK, DA))
    o_b, hgrn_p = _hgrn(hg.reshape(Bp, S, 4 * DA), lb, g_norm_b[0], jnp.zeros((Bp, H, hd, hd), F32),
                        chunk=256, blk=32)
    x1, h2 = _merge_out(o_a.reshape(Mp, DA), o_b.reshape(Mp, DA), gates, xp, w_ba, w_bb, w_o, norm_ffn[0], tm=256)
    zeros_e = jnp.zeros((Bp, 1, F), F32)
    x2, g_tail = _ffn(h2, x1, w_fg, w_fu, w_fd, conv_w[0], conv_b[0], zeros_e, zeros_e,
                      tm=512, tf=512, rows_per_seq=S, period=None)
    y_prompt = _ple(x2, p_prompt[0].reshape(Mp, -1), norm_ple[0], w_pg, w_pp, tm=256).reshape(Bp, S, D)
    k_prompt = k.reshape(1, Bp, S, H, hd)
    v_prompt = v.reshape(1, Bp, S, H, hd)
    tps = S // 512
    conv_prompt = g_tail[tps - 1::tps, 8 - (CONV_W - 1):, :][None]

    Ms = Bd * T
    R = 8
    xs = x_sample.reshape(Ms, D)
    cos_s, sin_s = _rope_tables(n_pages * page + jnp.arange(T))
    cos_s, sin_s = jnp.tile(cos_s, (Bd, 1)), jnp.tile(sin_s, (Bd, 1))
    hs = _rmsnorm(xs, norm_mix[0], Ms)
    qs, (ks, _), vs, _, hgs, gates_s = _in_proj(
        hs, w_in_b, q_norm[0], k_norm[0], cos_s, sin_s, tm=Ms, rope_tiles=1, with_mean=False, q_dtype=F32, q_scale=1.0)
    pool_k = cache_k.reshape(n_pool, page, H, hd)
    kmean_s = _pool_block_means(pool_k, page_table, ppb)
    q_pad = _pad_rows(qs.reshape(Bd, T, DA), R)
    sel = _decode_select(q_pad, jnp.transpose(kmean_s, (0, 2, 1, 3)))[:, :, :T, :MOBA_TOPK]
    blk_pages = page_table.reshape(Bd, n_pages // ppb, ppb)
    sel_pages = blk_pages[jnp.arange(Bd)[:, None, None, None], sel]
    o_as = _decode_attn(q_pad, _pad_rows(ks.reshape(Bd, T, DA), R), _pad_rows(vs.reshape(Bd, T, DA), R),
                        pool_k, cache_v.reshape(n_pool, page, H, hd),
                        sel_pages.reshape(-1).astype(jnp.int32), n_tok=T)[:, :T]
    RH = 16
    o_bs, hgrn_s = _hgrn_decode(_pad_rows(hgs.reshape(Bd, T, 4 * DA), RH), lb, g_norm_b[0], state_hgrn[0], valid=T)
    x1s, h2s = _merge_out(o_as.reshape(Ms, DA).astype(BF16), o_bs[:, :T].reshape(Ms, DA), gates_s, xs,
                          w_ba, w_bb, w_o, norm_ffn[0], tm=Ms)
    buf = state_conv[0]
    zrow = jnp.zeros((Bd, 1, F), F32)
    e1 = jnp.concatenate([buf[:, 1:2], jnp.tile(zrow, (1, T - 1, 1))], axis=1).reshape(Ms, F)
    e2 = jnp.concatenate([buf[:, 0:1], buf[:, 1:2], jnp.tile(zrow, (1, T - 2, 1))], axis=1).reshape(Ms, F)
    x2s, g_s = _ffn(h2s, x1s, w_fg, w_fu, w_fd, conv_w[0], conv_b[0], e1, e2,
                    tm=Ms, tf=512, rows_per_seq=T, period=T)
    y_sample = _ple(x2s, p_sample[0].reshape(Ms, -1), norm_ple[0], w_pg, w_pp, tm=Ms).reshape(Bd, T, D)
    k_sample = ks.reshape(1, Bd, T, H, hd)
    v_sample = vs.reshape(1, Bd, T, H, hd)
    conv_sample = g_s.reshape(Bd, T, F)[:, T - (CONV_W - 1):][None]

    return (y_prompt, y_sample, k_prompt, v_prompt, hgrn_p[None], conv_prompt,
            k_sample, v_sample, hgrn_s[None], conv_sample)
```

```python
import functools

import jax
import jax.numpy as jnp
from jax import lax
from jax.experimental import pallas as pl
from jax.experimental.pallas import tpu as pltpu

F32 = jnp.float32
BF16 = jnp.bfloat16
EPS = 1e-6
NEG_INF = -1e30
ROPE_THETA = 10000.0
LOG2E = 1.4426950408889634

HEAD = 128
N_HEADS = 8
MOBA_BLOCK = 256
MOBA_TOPK = 3
CONV_W = 3

VMEM_LIMIT = 52 * 1024 * 1024


def _cparams(*sem):
    return pltpu.CompilerParams(dimension_semantics=sem, vmem_limit_bytes=VMEM_LIMIT)


def _dot(a, b):
    return jnp.dot(a, b, preferred_element_type=F32)


def _dot_nt(a, b):
    return lax.dot_general(a, b, (((1,), (1,)), ((), ())), preferred_element_type=F32)


def _dot_tn(a, b):
    return lax.dot_general(a, b, (((0,), (0,)), ((), ())), preferred_element_type=F32)


def _sigmoid(x):
    return 1.0 / (1.0 + jnp.exp(-x))


def _silu(x):
    return x * _sigmoid(x)


def _rmsnorm_kernel(x_ref, g_ref, o_ref):
    x = x_ref[...]
    ms = jnp.mean(x * x, axis=-1, keepdims=True)
    o_ref[...] = (x * lax.rsqrt(ms + EPS) * g_ref[...]).astype(o_ref.dtype)


def _rmsnorm(x, gain, tm):
    M, D = x.shape
    return pl.pallas_call(
        _rmsnorm_kernel,
        grid=(M // tm,),
        in_specs=[pl.BlockSpec((tm, D), lambda i: (i, 0)), pl.BlockSpec((1, D), lambda i: (0, 0))],
        out_specs=pl.BlockSpec((tm, D), lambda i: (i, 0)),
        out_shape=jax.ShapeDtypeStruct((M, D), BF16),
        compiler_params=_cparams("parallel"),
        name="rmsnorm",
    )(x, gain.reshape(1, D))


def _mm_kernel(*refs, n_extra, epilogue):
    a_ref, w_ref = refs[:2]
    extras = refs[2:2 + n_extra]
    outs = refs[2 + n_extra:]
    acc = _dot(a_ref[...], w_ref[...])
    res = epilogue(acc, *[e[...] for e in extras])
    for o, r in zip(outs, res):
        o[...] = r.reshape(o.shape).astype(o.dtype)


def _matmul(a, w, *, col0, n_cols, tm, tn, epilogue, extras=(), extra_specs=(), out_shapes, out_specs, name):
    M, K = a.shape
    cb = col0 // tn
    return pl.pallas_call(
        functools.partial(_mm_kernel, n_extra=len(extras), epilogue=epilogue),
        grid=(n_cols // tn, M // tm),
        in_specs=[pl.BlockSpec((tm, K), lambda j, i: (i, 0)),
                  pl.BlockSpec((K, tn), lambda j, i: (0, cb + j))] + list(extra_specs),
        out_specs=out_specs,
        out_shape=out_shapes,
        compiler_params=_cparams("parallel", "parallel"),
        name=name,
    )(a, w, *extras)


def _qk_norm_rope(acc, gain, cos, sin):
    outs = []
    for h in range(acc.shape[1] // HEAD):
        z = acc[:, h * HEAD:(h + 1) * HEAD]
        y = z * lax.rsqrt(jnp.mean(z * z, axis=-1, keepdims=True) + EPS) * gain
        outs.append(y * cos + pltpu.roll(y, HEAD // 2, 1) * sin)
    return jnp.concatenate(outs, axis=1)


def _q_epilogue(acc, gain, cos, sin, *, q_scale):
    return [_qk_norm_rope(acc, gain, cos, sin) * q_scale]


def _k_epilogue(acc, gain, cos, sin, *, with_mean):
    k = _qk_norm_rope(acc, gain, cos, sin)
    if not with_mean:
        return [k, k]
    nblk = k.shape[0] // MOBA_BLOCK
    means = [jnp.mean(k[n * MOBA_BLOCK:(n + 1) * MOBA_BLOCK], axis=0, keepdims=True) for n in range(nblk)]
    return [k, k, jnp.concatenate(means, axis=0)]


def _rope_tables(pos):
    half = HEAD // 2
    inv_freq = jnp.power(ROPE_THETA, -jnp.arange(half, dtype=F32) * (2.0 / HEAD))
    ang = pos.astype(F32)[:, None] * inv_freq[None, :]
    c, s = jnp.cos(ang), jnp.sin(ang)
    return jnp.concatenate([c, c], axis=1), jnp.concatenate([-s, s], axis=1)


def _in_proj(h, w_in, q_gain, k_gain, cos, sin, *, tm, rope_tiles, with_mean, q_dtype, q_scale):
    M = h.shape[0]
    DA = N_HEADS * HEAD
    row_tile = lambda j, i: (i, j)
    rope_spec = pl.BlockSpec((tm, HEAD), lambda j, i: (i % rope_tiles, 0))
    gain_spec = pl.BlockSpec((1, HEAD), lambda j, i: (0, 0))
    tile = pl.BlockSpec((tm, DA), row_tile)
    (q,) = _matmul(h, w_in, col0=0, n_cols=DA, tm=tm, tn=DA,
                   epilogue=functools.partial(_q_epilogue, q_scale=q_scale),
                   extras=(q_gain.reshape(1, HEAD), cos, sin), extra_specs=(gain_spec, rope_spec, rope_spec),
                   out_shapes=[jax.ShapeDtypeStruct((M, DA), q_dtype)], out_specs=[tile], name="proj_q")
    k_shapes = [jax.ShapeDtypeStruct((M, DA), F32), jax.ShapeDtypeStruct((M, DA), BF16)]
    k_specs = [tile, tile]
    if with_mean:
        nb = tm // MOBA_BLOCK
        k_shapes.append(jax.ShapeDtypeStruct((M // tm, nb, DA), F32))
        k_specs.append(pl.BlockSpec((1, nb, DA), lambda j, i: (i, 0, j)))
    k_out = _matmul(h, w_in, col0=DA, n_cols=DA, tm=tm, tn=DA,
                    epilogue=functools.partial(_k_epilogue, with_mean=with_mean),
                    extras=(k_gain.reshape(1, HEAD), cos, sin), extra_specs=(gain_spec, rope_spec, rope_spec),
                    out_shapes=k_shapes, out_specs=k_specs, name="proj_k")
    v, v_bf = _matmul(h, w_in, col0=2 * DA, n_cols=DA, tm=tm, tn=DA, epilogue=lambda acc: [acc, acc],
                      out_shapes=[jax.ShapeDtypeStruct((M, DA), F32), jax.ShapeDtypeStruct((M, DA), BF16)],
                      out_specs=[tile, tile], name="proj_v")
    (hg,) = _matmul(h, w_in, col0=3 * DA, n_cols=4 * DA, tm=tm, tn=DA, epilogue=lambda acc: [acc],
                    out_shapes=[jax.ShapeDtypeStruct((M, 4 * DA), F32)], out_specs=[tile], name="proj_hgrn")
    (gates,) = _matmul(h, w_in, col0=7 * DA, n_cols=4 * DA, tm=tm, tn=DA, epilogue=lambda acc: [_sigmoid(acc)],
                       out_shapes=[jax.ShapeDtypeStruct((M, 4 * DA), F32)], out_specs=[tile], name="proj_gates")
    return q, k_out, v, v_bf, hg, gates


MOBA_STEP_BLOCKS = 2


def _page_block_means(pages, o_ref):
    ppb = len(pages) // o_ref.shape[1]
    rows = pages[0].shape[1]
    for b in range(o_ref.shape[1]):
        s = jnp.sum(pages[b * ppb][0], axis=0)
        for p in range(1, ppb):
            s = s + jnp.sum(pages[b * ppb + p][0], axis=0)
        o_ref[0, b] = s * (1.0 / (ppb * rows))


def _moba_prompt_kernel(pt_ref, q_ref, k_ref, v_ref, km_ref, ind_ref, *rest):
    *pages, o_ref, pool_mean_ref, s_a, s_b = rest
    _page_block_means(pages, pool_mean_ref)
    cur = pl.program_id(2)
    B = MOBA_BLOCK
    q = q_ref[0]
    nb = km_ref.shape[1]
    lane = lax.broadcasted_iota(jnp.int32, (1, HEAD), 1)
    ones_col = jnp.where(lane == 0, 1.0, 0.0).astype(BF16)

    def with_ones(v):
        return jnp.concatenate([v, jnp.broadcast_to(ones_col, v.shape)], axis=1)

    own = pl.multiple_of(cur * B, B)
    s = _dot_nt(q, k_ref[0, pl.ds(own, B), :])
    row = lax.broadcasted_iota(jnp.int32, (B, B), 0)
    col = lax.broadcasted_iota(jnp.int32, (B, B), 1)
    s = jnp.where(col <= row, s, NEG_INF)
    m = jnp.max(s, axis=-1, keepdims=True)
    acc = _dot(jnp.exp2(s - m).astype(BF16), with_ones(v_ref[0, pl.ds(own, B), :]))

    km = km_ref[0]
    km_hi = km.astype(BF16)
    km_lo = (km - km_hi.astype(F32)).astype(BF16)
    g = _dot_nt(km_hi, q) + _dot_nt(km_lo, q)
    blk = lax.broadcasted_iota(jnp.int32, (nb, B), 0)
    g = jnp.where(blk < cur, g, NEG_INF)
    sel = jnp.zeros((nb, B), F32)
    for _ in range(MOBA_TOPK):
        mx = jnp.max(g, axis=0, keepdims=True)
        idx = jnp.min(jnp.where(g == mx, blk, nb), axis=0, keepdims=True)
        hit = blk == idx
        sel = jnp.where(hit, 1.0, sel)
        g = jnp.where(hit, -jnp.inf, g)
    sel = jnp.where(blk < cur, sel, 0.0).astype(BF16)
    eye = (lax.broadcasted_iota(jnp.int32, (nb, HEAD), 0) == lax.broadcasted_iota(jnp.int32, (nb, HEAD), 1))
    sel_rows = _dot_tn(sel, jnp.where(eye, 1.0, 0.0).astype(BF16))
    q_aug = jnp.concatenate([q, jnp.where(sel_rows > 0.5, 0.0, NEG_INF).astype(BF16)], axis=1)

    W = MOBA_STEP_BLOCKS * B
    last_step = nb // MOBA_STEP_BLOCKS - 1
    n_trips = (cur + 2 * MOBA_STEP_BLOCKS - 1) // (2 * MOBA_STEP_BLOCKS)

    def scores(n):
        start = pl.multiple_of(n * W, W)
        k_aug = jnp.concatenate([k_ref[0, pl.ds(start, W), :], ind_ref[pl.ds(start, W), :]], axis=1)
        return _dot_nt(q_aug, k_aug)

    def softmax_step(s_ref, n, m, acc):
        s = s_ref[...]
        m_new = jnp.maximum(m, jnp.max(s, axis=-1, keepdims=True))
        p = jnp.exp2(s - m_new).astype(BF16)
        start = pl.multiple_of(n * W, W)
        return m_new, jnp.exp2(m - m_new) * acc + _dot(p, with_ones(v_ref[0, pl.ds(start, W), :]))

    def body(j, carry):
        m, acc = carry
        s_b[...] = scores(2 * j + 1)
        m, acc = softmax_step(s_a, 2 * j, m, acc)
        s_a[...] = scores(jnp.minimum(2 * j + 2, last_step))
        return softmax_step(s_b, 2 * j + 1, m, acc)

    s_a[...] = scores(0)
    m, acc = lax.fori_loop(0, n_trips, body, (m, acc))
    o_ref[0] = (acc[:, :HEAD] / acc[:, HEAD:HEAD + 1]).astype(o_ref.dtype)


def _moba_prompt(q, k, v, kmean, pool, page_table, ppb):
    Bn, S, _ = q.shape
    nb = S // MOBA_BLOCK
    assert nb % (2 * MOBA_STEP_BLOCKS) == 0 and nb <= HEAD
    Bd, n_pages = page_table.shape
    _, page, H, hd = pool.shape
    n_steps = Bn * N_HEADS * nb
    P = Bd * n_pages // n_steps
    assert P * n_steps == Bd * n_pages and P % ppb == 0 and n_pages % P == 0
    G = n_pages // P

    def page_map(b, h, i, pt, *, p):
        t = (b * N_HEADS + h) * nb + i
        return (pt[t // G, (t % G) * P + p], 0, 0, 0)

    def mean_map(b, h, i, pt):
        t = (b * N_HEADS + h) * nb + i
        return (t // G, t % G, 0, 0)

    s_buf = pltpu.VMEM((MOBA_BLOCK, MOBA_STEP_BLOCKS * MOBA_BLOCK), F32)
    ind = (jnp.arange(S)[:, None] // MOBA_BLOCK == jnp.arange(HEAD)[None, :]).astype(BF16)
    return pl.pallas_call(
        _moba_prompt_kernel,
        grid_spec=pltpu.PrefetchScalarGridSpec(
            num_scalar_prefetch=1,
            grid=(Bn, N_HEADS, nb),
            in_specs=[pl.BlockSpec((1, MOBA_BLOCK, HEAD), lambda b, h, i, pt: (b, i, h)),
                      pl.BlockSpec((1, S, HEAD), lambda b, h, i, pt: (b, 0, h)),
                      pl.BlockSpec((1, S, HEAD), lambda b, h, i, pt: (b, 0, h)),
                      pl.BlockSpec((1, nb, HEAD), lambda b, h, i, pt: (b, 0, h)),
                      pl.BlockSpec((S, HEAD), lambda b, h, i, pt: (0, 0))]
                     + [pl.BlockSpec((1, page, H, hd), functools.partial(page_map, p=p)) for p in range(P)],
            out_specs=[pl.BlockSpec((1, MOBA_BLOCK, HEAD), lambda b, h, i, pt: (b, i, h)),
                       pl.BlockSpec((1, P // ppb, H, hd), mean_map)],
            scratch_shapes=[s_buf, s_buf]),
        out_shape=[jax.ShapeDtypeStruct(q.shape, BF16), jax.ShapeDtypeStruct((Bd, n_pages // ppb, H, hd), F32)],
        compiler_params=_cparams("arbitrary", "arbitrary", "arbitrary"),
        name="moba_prompt",
    )(page_table, q, k, v, kmean, ind, *([pool] * P))


def _hgrn_chunk(q_raw, z, v, gate, lb, gain, st, *, blk, valid):
    C = z.shape[0]
    logf = jnp.log(lb + (1.0 - lb) * _sigmoid(z))
    kk = (1.0 - lb) * _sigmoid(-z)
    rowi = lax.broadcasted_iota(jnp.int32, (C, 1), 0)
    if valid < C:
        logf = jnp.where(rowi < valid, logf, 0.0)
        kk = jnp.where(rowi < valid, kk, 0.0)
    q = _silu(q_raw)
    v = v.astype(BF16)

    tri = lax.broadcasted_iota(jnp.int32, (C, C), 0) >= lax.broadcasted_iota(jnp.int32, (C, C), 1)
    tri_bf = jnp.where(tri, 1.0, 0.0).astype(BF16)
    g1 = logf.astype(BF16)
    r1 = logf - g1.astype(F32)
    g2 = r1.astype(BF16)
    g3 = (r1 - g2.astype(F32)).astype(BF16)
    A = _dot(tri_bf, g1) + _dot(tri_bf, g2) + _dot(tri_bf, g3)
    a_last = A[C - 1:C, :]

    o_inter = _dot_nt((q * jnp.exp(A)).astype(BF16), st.astype(BF16))
    k_dec = (kk * jnp.exp(a_last - A)).astype(BF16)
    st_new = st * jnp.exp(a_last) + _dot_tn(v, k_dec)

    pieces = []
    for j in range(C // blk):
        lo, hi = j * blk, (j + 1) * blk
        a_ref_row = A[lo + blk // 2 - 1:lo + blk // 2, :]
        qt = (q[lo:hi] * jnp.exp(A[lo:hi] - a_ref_row)).astype(BF16)
        kt = (kk * jnp.exp(jnp.where(rowi < hi, a_ref_row - A, NEG_INF))).astype(BF16)
        pieces.append(_dot_nt(qt, kt))
    scores = pieces[0] if len(pieces) == 1 else jnp.concatenate(pieces, axis=0)
    scores = jnp.where(tri, scores, 0.0).astype(BF16)
    o = _dot(scores, v) + o_inter

    y = o * lax.rsqrt(jnp.mean(o * o, axis=-1, keepdims=True) + EPS) * gain
    return y * _silu(gate), st_new


def _hgrn_kernel(q_ref, z_ref, v_ref, g_ref, lb_ref, gn_ref, s0_ref, o_ref, s_ref, st_scr, *, blk):
    @pl.when(pl.program_id(2) == 0)
    def _():
        st_scr[...] = s0_ref[0, 0].T

    o, st_new = _hgrn_chunk(q_ref[0], z_ref[0], v_ref[0], g_ref[0], lb_ref[...], gn_ref[...], st_scr[...],
                            blk=blk, valid=q_ref.shape[1])
    st_scr[...] = st_new
    s_ref[0, 0] = st_new.T
    o_ref[0] = o.astype(o_ref.dtype)


def _hgrn_decode_kernel(hg_ref, lb_ref, gn_ref, s0_ref, o_ref, s_ref, *, valid):
    D = N_HEADS * HEAD
    for h in range(N_HEADS):
        cols = [hg_ref[0, :, g * D + h * HEAD:g * D + (h + 1) * HEAD] for g in range(4)]
        o, st_new = _hgrn_chunk(*cols, lb_ref[:, h * HEAD:(h + 1) * HEAD], gn_ref[...], s0_ref[0, h].T,
                                blk=hg_ref.shape[1], valid=valid)
        s_ref[0, h] = st_new.T
        o_ref[0, :, h * HEAD:(h + 1) * HEAD] = o.astype(o_ref.dtype)


def _hgrn_decode(hg, lb, g_norm, s0, *, valid):
    Bn, R, _ = hg.shape
    D = N_HEADS * HEAD
    state = pl.BlockSpec((1, N_HEADS, HEAD, HEAD), lambda b: (b, 0, 0, 0))
    return pl.pallas_call(
        functools.partial(_hgrn_decode_kernel, valid=valid),
        grid=(Bn,),
        in_specs=[pl.BlockSpec((1, R, 4 * D), lambda b: (b, 0, 0)), pl.BlockSpec((1, D), lambda b: (0, 0)),
                  pl.BlockSpec((1, HEAD), lambda b: (0, 0)), state],
        out_specs=[pl.BlockSpec((1, R, D), lambda b: (b, 0, 0)), state],
        out_shape=[jax.ShapeDtypeStruct((Bn, R, D), BF16), jax.ShapeDtypeStruct(s0.shape, F32)],
        compiler_params=_cparams("parallel"),
        name="hgrn_decode",
    )(hg, lb.reshape(1, D), g_norm.reshape(1, HEAD), s0)


def _hgrn(hg, lb, g_norm, s0, *, chunk, blk):
    Bn, T, _ = hg.shape
    H = N_HEADS
    col = lambda g: pl.BlockSpec((1, chunk, HEAD), lambda b, h, c: (b, c, g * H + h))
    vec = pl.BlockSpec((1, HEAD), lambda b, h, c: (0, h))
    state = pl.BlockSpec((1, 1, HEAD, HEAD), lambda b, h, c: (b, h, 0, 0))
    return pl.pallas_call(
        functools.partial(_hgrn_kernel, blk=blk),
        grid=(Bn, H, T // chunk),
        in_specs=[col(0), col(1), col(2), col(3), vec, pl.BlockSpec((1, HEAD), lambda b, h, c: (0, 0)), state],
        out_specs=[pl.BlockSpec((1, chunk, HEAD), lambda b, h, c: (b, c, h)), state],
        out_shape=[jax.ShapeDtypeStruct((Bn, T, H * HEAD), BF16), jax.ShapeDtypeStruct(s0.shape, F32)],
        scratch_shapes=[pltpu.VMEM((HEAD, HEAD), F32)],
        compiler_params=_cparams("parallel", "parallel", "arbitrary"),
        name="hgrn",
    )(hg, hg, hg, hg, lb.reshape(1, H * HEAD), g_norm.reshape(1, HEAD), s0)


def _merge_kernel(oa_ref, ob_ref, gates_ref, x_ref, wa_ref, wb_ref, wo_ref, gain_ref, x1_ref, h2_ref):
    D = x_ref.shape[1]
    mix = gates_ref[:, :D] * _dot(oa_ref[...], wa_ref[...]) + gates_ref[:, D:] * _dot(ob_ref[...], wb_ref[...])
    x1 = x_ref[...] + _dot(mix.astype(BF16), wo_ref[...])
    x1_ref[...] = x1
    h2_ref[...] = (x1 * lax.rsqrt(jnp.mean(x1 * x1, axis=-1, keepdims=True) + EPS) * gain_ref[...]).astype(h2_ref.dtype)


def _resident(shape):
    return pl.BlockSpec(shape, lambda *_: (0,) * len(shape), pipeline_mode=pl.Buffered(1))


def _merge_out(o_a, o_b, gates, x, w_ba, w_bb, w_out, norm_ffn, *, tm):
    M, D = x.shape
    DA = o_a.shape[1]
    rows = lambda w: pl.BlockSpec((tm, w), lambda i: (i, 0))
    return pl.pallas_call(
        _merge_kernel,
        grid=(M // tm,),
        in_specs=[rows(DA), rows(DA), rows(2 * D), rows(D), _resident((DA, D)), _resident((DA, D)), _resident((D, D)),
                  _resident((1, D))],
        out_specs=[rows(D), rows(D)],
        out_shape=[jax.ShapeDtypeStruct((M, D), F32), jax.ShapeDtypeStruct((M, D), BF16)],
        compiler_params=_cparams("parallel"),
        name="merge_out",
    )(o_a, o_b, gates, x, w_ba, w_bb, w_out, norm_ffn.reshape(1, D))


def _ffn_kernel(h_ref, wg_ref, wu_ref, wd_ref, cw_ref, cb_ref, x_ref, e1_ref, e2_ref,
                y_ref, g_ref, tail_scr, *, tiles_per_seq, period):
    i, f = pl.program_id(0), pl.program_id(1)
    tm = h_ref.shape[0]

    @pl.when(f == 0)
    def _():
        y_ref[...] = x_ref[...]

    if period is None:
        @pl.when(i % tiles_per_seq == 0)
        def _():
            tail_scr[f, 0:1, :] = e2_ref[0]
            tail_scr[f, 1:2, :] = e1_ref[0]

    h = h_ref[...]
    g = _dot(h, wg_ref[...])
    u = _dot(h, wu_ref[...])
    rowi = lax.broadcasted_iota(jnp.int32, (tm, 1), 0)
    r1 = pltpu.roll(g, 1, 0)
    r2 = pltpu.roll(g, 2, 0)
    if period is None:
        p2 = tail_scr[f, 0:1, :]
        p1 = tail_scr[f, 1:2, :]
        g1 = jnp.where(rowi == 0, p1, r1)
        g2 = jnp.where(rowi == 0, p2, jnp.where(rowi == 1, p1, r2))
        tail_scr[f, 0:2, :] = g[tm - 2:tm]
    else:
        t = rowi % period
        g1 = jnp.where(t == 0, e1_ref[...], r1)
        g2 = jnp.where(t < 2, e2_ref[...], r2)
    g_ref[...] = g.reshape(g_ref.shape) if period is not None else g[tm - 8:tm].reshape(g_ref.shape)
    cw = cw_ref[...]
    c = cb_ref[...] + cw[0:1] * g2 + cw[1:2] * g1 + cw[2:3] * g
    y_ref[...] += _dot((_silu(c) * u).astype(BF16), wd_ref[...])


def _ffn(h2, x1, w_gate, w_up, w_down, conv_w, conv_b, e1, e2, *, tm, tf, rows_per_seq, period):
    M, D = x1.shape
    F = w_gate.shape[1]
    nf = F // tf
    if period is None:
        tps = rows_per_seq // tm
        e_spec = pl.BlockSpec((1, 1, tf), lambda i, f: (i // tps, 0, f))
        g_shape = jax.ShapeDtypeStruct((M // tm, 8, F), F32)
        g_spec = pl.BlockSpec((1, 8, tf), lambda i, f: (i, 0, f))
    else:
        tps = 1
        e_spec = pl.BlockSpec((tm, tf), lambda i, f: (i, f))
        g_shape = jax.ShapeDtypeStruct((M, F), F32)
        g_spec = pl.BlockSpec((tm, tf), lambda i, f: (i, f))
    return pl.pallas_call(
        functools.partial(_ffn_kernel, tiles_per_seq=tps, period=period),
        grid=(M // tm, nf),
        in_specs=[pl.BlockSpec((tm, D), lambda i, f: (i, 0)),
                  pl.BlockSpec((D, tf), lambda i, f: (0, f)),
                  pl.BlockSpec((D, tf), lambda i, f: (0, f)),
                  pl.BlockSpec((tf, D), lambda i, f: (f, 0)),
                  pl.BlockSpec((CONV_W, tf), lambda i, f: (0, f)),
                  pl.BlockSpec((1, tf), lambda i, f: (0, f)),
                  pl.BlockSpec((tm, D), lambda i, f: (i, 0)),
                  e_spec, e_spec],
        out_specs=[pl.BlockSpec((tm, D), lambda i, f: (i, 0)), g_spec],
        out_shape=[jax.ShapeDtypeStruct((M, D), F32), g_shape],
        scratch_shapes=[pltpu.VMEM((nf, 8, tf), F32)],
        compiler_params=_cparams("arbitrary", "arbitrary"),
        name="ffn",
    )(h2, w_gate, w_up, w_down, conv_w, conv_b.reshape(1, F), x1, e1, e2)


def _ple_kernel(x_ref, gain_ref, wg_ref, p_ref, wp_ref, o_ref):
    x = x_ref[...]
    hn = (x * lax.rsqrt(jnp.mean(x * x, axis=-1, keepdims=True) + EPS) * gain_ref[...]).astype(BF16)
    gate = _sigmoid(_dot(hn, wg_ref[...]))
    o_ref[...] = x + gate * _dot(p_ref[...].astype(BF16), wp_ref[...])


def _ple(x2, p, norm_ple, w_gate, w_proj, *, tm):
    M, D = x2.shape
    P = p.shape[1]
    return pl.pallas_call(
        _ple_kernel,
        grid=(M // tm,),
        in_specs=[pl.BlockSpec((tm, D), lambda i: (i, 0)), _resident((1, D)), _resident((D, D)),
                  pl.BlockSpec((tm, P), lambda i: (i, 0)), _resident((P, D))],
        out_specs=pl.BlockSpec((tm, D), lambda i: (i, 0)),
        out_shape=jax.ShapeDtypeStruct((M, D), F32),
        compiler_params=_cparams("parallel"),
        name="ple",
    )(x2, norm_ple.reshape(1, D), w_gate, p, w_proj)


def _decode_select_kernel(q_ref, km_ref, o_ref):
    nb = km_ref.shape[2]
    for h in range(N_HEADS):
        q = q_ref[0, :, h * HEAD:(h + 1) * HEAD].astype(BF16)
        km = km_ref[0, h]
        km_hi = km.astype(BF16)
        km_lo = (km - km_hi.astype(F32)).astype(BF16)
        g = _dot_nt(q, km_hi) + _dot_nt(q, km_lo)
        blk = lax.broadcasted_iota(jnp.int32, g.shape, 1)
        lane = lax.broadcasted_iota(jnp.int32, (g.shape[0], 128), 1)
        out = jnp.zeros((g.shape[0], 128), jnp.int32)
        for k in range(MOBA_TOPK):
            mx = jnp.max(g, axis=-1, keepdims=True)
            idx = jnp.min(jnp.where(g == mx, blk, nb), axis=-1, keepdims=True)
            out = jnp.where(lane == k, idx, out)
            g = jnp.where(blk == idx, -jnp.inf, g)
        o_ref[0, h] = out


def _decode_select(q_pad, kmean):
    Bd, R, _ = q_pad.shape
    nb = kmean.shape[2]
    return pl.pallas_call(
        _decode_select_kernel,
        grid=(Bd,),
        in_specs=[pl.BlockSpec((1, R, N_HEADS * HEAD), lambda s: (s, 0, 0)),
                  pl.BlockSpec((1, N_HEADS, nb, HEAD), lambda s: (s, 0, 0, 0))],
        out_specs=pl.BlockSpec((1, N_HEADS, R, 128), lambda s: (s, 0, 0, 0)),
        out_shape=jax.ShapeDtypeStruct((Bd, N_HEADS, R, 128), jnp.int32),
        compiler_params=_cparams("parallel"),
        name="decode_select",
    )(q_pad, kmean)


def _decode_attn_kernel(pg_ref, q_ref, kn_ref, vn_ref, pk_ref, pv_ref, o_ref, kbuf, vbuf, sem, *, n_tok, n_slab, scale):
    nh = pl.num_programs(1)
    step = pl.program_id(0) * nh + pl.program_id(1)
    n_steps = pl.num_programs(0) * nh
    slot = step % 2
    per_step = n_tok * n_slab

    def slab_copies(st, sl):
        head = st % nh
        copies = []
        for j in range(per_step):
            pg = pg_ref[st * per_step + j]
            copies.append(pltpu.make_async_copy(pk_ref.at[pg, :, head, :], kbuf.at[sl, j], sem.at[0, sl]))
            copies.append(pltpu.make_async_copy(pv_ref.at[pg, :, head, :], vbuf.at[sl, j], sem.at[1, sl]))
        return copies

    @pl.when(step == 0)
    def _():
        for c in slab_copies(step, slot):
            c.start()

    @pl.when(step + 1 < n_steps)
    def _():
        for c in slab_copies(step + 1, 1 - slot):
            c.start()

    for c in slab_copies(step, slot):
        c.wait()

    R = q_ref.shape[1]
    page = kbuf.shape[2]
    kn = kn_ref[0]
    vn = vn_ref[0]
    jrow = lax.broadcasted_iota(jnp.int32, (R, 1), 0)
    o_ref[...] = jnp.zeros(o_ref.shape, o_ref.dtype)
    for t in range(n_tok):
        ks = kbuf[slot, t * n_slab:(t + 1) * n_slab].reshape(n_slab * page, HEAD).astype(BF16)
        vs = vbuf[slot, t * n_slab:(t + 1) * n_slab].reshape(n_slab * page, HEAD).astype(BF16)
        qrow = q_ref[0, t:t + 1, :]
        q8 = jnp.broadcast_to(qrow, (R, HEAD)).astype(BF16)
        s_sel = _dot_nt(q8, ks)[0:1] * scale
        s_own = jnp.sum(kn * qrow, axis=-1, keepdims=True) * scale
        s_own = jnp.where(jrow <= t, s_own, NEG_INF)
        m = jnp.maximum(jnp.max(s_sel, axis=-1, keepdims=True), jnp.max(s_own, axis=0, keepdims=True))
        p_sel = jnp.exp(s_sel - m)
        p_own = jnp.exp(s_own - m)
        l = jnp.sum(p_sel, axis=-1, keepdims=True) + jnp.sum(p_own, axis=0, keepdims=True)
        o = _dot(jnp.broadcast_to(p_sel, (R, p_sel.shape[1])).astype(BF16), vs)[0:1]
        o = o + jnp.sum(p_own * vn, axis=0, keepdims=True)
        o_ref[0, t:t + 1, :] = o / l


def _decode_attn(q_pad, kn_pad, vn_pad, pool_k, pool_v, sel_pages, *, n_tok):
    Bd, R, _ = q_pad.shape
    page = pool_k.shape[1]
    n_slab = sel_pages.shape[0] // (Bd * N_HEADS * n_tok)
    row = pl.BlockSpec((1, R, HEAD), lambda s, h, pg: (s, 0, h))
    hbm = pl.BlockSpec(memory_space=pl.ANY)
    slabs = pltpu.VMEM((2, n_tok * n_slab, page, HEAD), F32)
    return pl.pallas_call(
        functools.partial(_decode_attn_kernel, n_tok=n_tok, n_slab=n_slab, scale=HEAD ** -0.5),
        grid_spec=pltpu.PrefetchScalarGridSpec(
            num_scalar_prefetch=1, grid=(Bd, N_HEADS), in_specs=[row, row, row, hbm, hbm], out_specs=row,
            scratch_shapes=[slabs, slabs, pltpu.SemaphoreType.DMA((2, 2))]),
        out_shape=jax.ShapeDtypeStruct(q_pad.shape, F32),
        compiler_params=_cparams("arbitrary", "arbitrary"),
        name="decode_attn",
    )(sel_pages, q_pad, kn_pad, vn_pad, pool_k, pool_v)


def _pad_rows(a, rows):
    return jnp.pad(a, ((0, 0), (0, rows - a.shape[1]), (0, 0)))


def kernel(x_prompt, x_sample, cache_k, cache_v, state_hgrn, state_conv, page_table, p_prompt, p_sample,
           norm_mix, w_in, q_norm, k_norm, lb_logits, g_norm_b, w_branch_a, w_branch_b, w_out,
           norm_ffn, w_ffn_gate, w_ffn_up, conv_w, conv_b, w_ffn_down, norm_ple, w_ple_gate, w_ple_proj):
    Bp, S, D = x_prompt.shape
    Bd, T, _ = x_sample.shape
    depth, n_pool, page, H, hd = cache_k.shape
    n_pages = page_table.shape[1]
    F = w_ffn_gate.shape[-1]
    DA = H * hd
    ppb = MOBA_BLOCK // page
    assert depth == 1 and H == N_HEADS and hd == HEAD
    assert (n_pages * page) % MOBA_BLOCK == 0, "past tokens must fill whole attention blocks"
    assert n_pages // ppb >= MOBA_TOPK and S % MOBA_BLOCK == 0 and T <= 8

    bf = lambda w: w[0].astype(BF16)
    w_in_b, w_ba, w_bb, w_o = bf(w_in), bf(w_branch_a), bf(w_branch_b), bf(w_out)
    w_fg, w_fu, w_fd, w_pg, w_pp = bf(w_ffn_gate), bf(w_ffn_up), bf(w_ffn_down), bf(w_ple_gate), bf(w_ple_proj)
    lb = jnp.cumsum(jax.nn.softmax(lb_logits.astype(F32), axis=0), axis=0)[0]

    Mp = Bp * S
    tm = 512
    xp = x_prompt.reshape(Mp, D)
    cos_p, sin_p = _rope_tables(jnp.arange(S))
    h = _rmsnorm(xp, norm_mix[0], tm)
    q, (k, k_bf, kmean), v, v_bf, hg, gates = _in_proj(
        h, w_in_b, q_norm[0], k_norm[0], cos_p, sin_p, tm=tm, rope_tiles=S // tm, with_mean=True, q_dtype=BF16,
        q_scale=HEAD ** -0.5 * LOG2E)
    pool_k = cache_k.reshape(n_pool, page, H, hd)
    o_a, kmean_s = _moba_prompt(q.reshape(Bp, S, DA), k_bf.reshape(Bp, S, DA), v_bf.reshape(Bp, S, DA),
                                kmean.reshape(Bp, S // MOBA_BLOCK, DA), pool_k, page_table, ppb)
    o_b, hgrn_p = _hgrn(hg.reshape(Bp, S, 4 * DA), lb, g_norm_b[0], jnp.zeros((Bp, H, hd, hd), F32),
                        chunk=256, blk=32)
    x1, h2 = _merge_out(o_a.reshape(Mp, DA), o_b.reshape(Mp, DA), gates, xp, w_ba, w_bb, w_o, norm_ffn[0], tm=256)
    zeros_e = jnp.zeros((Bp, 1, F), F32)
    x2, g_tail = _ffn(h2, x1, w_fg, w_fu, w_fd, conv_w[0], conv_b[0], zeros_e, zeros_e,
                      tm=512, tf=512, rows_per_seq=S, period=None)
    y_prompt = _ple(x2, p_prompt[0].reshape(Mp, -1), norm_ple[0], w_pg, w_pp, tm=256).reshape(Bp, S, D)
    k_prompt = k.reshape(1, Bp, S, H, hd)
    v_prompt = v.reshape(1, Bp, S, H, hd)
    tps = S // 512
    conv_prompt = g_tail[tps - 1::tps, 8 - (CONV_W - 1):, :][None]

    Ms = Bd * T
    R = 8
    xs = x_sample.reshape(Ms, D)
    cos_s, sin_s = _rope_tables(n_pages * page + jnp.arange(T))
    cos_s, sin_s = jnp.tile(cos_s, (Bd, 1)), jnp.tile(sin_s, (Bd, 1))
    hs = _rmsnorm(xs, norm_mix[0], Ms)
    qs, (ks, _), vs, _, hgs, gates_s = _in_proj(
        hs, w_in_b, q_norm[0], k_norm[0], cos_s, sin_s, tm=Ms, rope_tiles=1, with_mean=False, q_dtype=F32, q_scale=1.0)
    q_pad = _pad_rows(qs.reshape(Bd, T, DA), R)
    sel = _decode_select(q_pad, jnp.transpose(kmean_s, (0, 2, 1, 3)))[:, :, :T, :MOBA_TOPK]
    blk_pages = page_table.reshape(Bd, n_pages // ppb, ppb)
    sel_pages = blk_pages[jnp.arange(Bd)[:, None, None, None], sel]
    o_as = _decode_attn(q_pad, _pad_rows(ks.reshape(Bd, T, DA), R), _pad_rows(vs.reshape(Bd, T, DA), R),
                        pool_k, cache_v.reshape(n_pool, page, H, hd),
                        sel_pages.reshape(-1).astype(jnp.int32), n_tok=T)[:, :T]
    RH = 16
    o_bs, hgrn_s = _hgrn_decode(_pad_rows(hgs.reshape(Bd, T, 4 * DA), RH), lb, g_norm_b[0], state_hgrn[0], valid=T)
    x1s, h2s = _merge_out(o_as.reshape(Ms, DA).astype(BF16), o_bs[:, :T].reshape(Ms, DA), gates_s, xs,
                          w_ba, w_bb, w_o, norm_ffn[0], tm=Ms)
    buf = state_conv[0]
    zrow = jnp.zeros((Bd, 1, F), F32)
    e1 = jnp.concatenate([buf[:, 1:2], jnp.tile(zrow, (1, T - 1, 1))], axis=1).reshape(Ms, F)
    e2 = jnp.concatenate([buf[:, 0:1], buf[:, 1:2], jnp.tile(zrow, (1, T - 2, 1))], axis=1).reshape(Ms, F)
    x2s, g_s = _ffn(h2s, x1s, w_fg, w_fu, w_fd, conv_w[0], conv_b[0], e1, e2,
                    tm=Ms, tf=512, rows_per_seq=T, period=T)
    y_sample = _ple(x2s, p_sample[0].reshape(Ms, -1), norm_ple[0], w_pg, w_pp, tm=Ms).reshape(Bd, T, D)
    k_sample = ks.reshape(1, Bd, T, H, hd)
    v_sample = vs.reshape(1, Bd, T, H, hd)
    conv_sample = g_s.reshape(Bd, T, F)[:, T - (CONV_W - 1):][None]

    return (y_prompt, y_sample, k_prompt, v_prompt, hgrn_p[None], conv_prompt,
            k_sample, v_sample, hgrn_s[None], conv_sample)
```

```python
import functools

import jax
import jax.numpy as jnp
from jax import lax
from jax.experimental import pallas as pl
from jax.experimental.pallas import tpu as pltpu

F32 = jnp.float32
BF16 = jnp.bfloat16
EPS = 1e-6
NEG_INF = -1e30
ROPE_THETA = 10000.0
LOG2E = 1.4426950408889634

HEAD = 128
N_HEADS = 8
MOBA_BLOCK = 256
MOBA_TOPK = 3
CONV_W = 3

VMEM_LIMIT = 52 * 1024 * 1024


def _cparams(*sem):
    return pltpu.CompilerParams(dimension_semantics=sem, vmem_limit_bytes=VMEM_LIMIT)


def _dot(a, b):
    return jnp.dot(a, b, preferred_element_type=F32)


def _dot_nt(a, b):
    return lax.dot_general(a, b, (((1,), (1,)), ((), ())), preferred_element_type=F32)


def _dot_tn(a, b):
    return lax.dot_general(a, b, (((0,), (0,)), ((), ())), preferred_element_type=F32)


def _sigmoid(x):
    return 1.0 / (1.0 + jnp.exp(-x))


def _silu(x):
    return x * _sigmoid(x)


def _rmsnorm_kernel(x_ref, g_ref, o_ref):
    x = x_ref[...]
    ms = jnp.mean(x * x, axis=-1, keepdims=True)
    o_ref[...] = (x * lax.rsqrt(ms + EPS) * g_ref[...]).astype(o_ref.dtype)


def _rmsnorm(x, gain, tm):
    M, D = x.shape
    return pl.pallas_call(
        _rmsnorm_kernel,
        grid=(M // tm,),
        in_specs=[pl.BlockSpec((tm, D), lambda i: (i, 0)), pl.BlockSpec((1, D), lambda i: (0, 0))],
        out_specs=pl.BlockSpec((tm, D), lambda i: (i, 0)),
        out_shape=jax.ShapeDtypeStruct((M, D), BF16),
        compiler_params=_cparams("parallel"),
        name="rmsnorm",
    )(x, gain.reshape(1, D))


def _mm_kernel(*refs, n_extra, epilogue, row_parts):
    a_ref, w_ref = refs[:2]
    extras = refs[2:2 + n_extra]
    outs = refs[2 + n_extra:]
    tm = a_ref.shape[0]
    for part in range(row_parts):
        def rows_of(n_rows):
            return slice(part * (n_rows // row_parts), (part + 1) * (n_rows // row_parts))
        acc = _dot(a_ref[rows_of(tm), :], w_ref[...])
        res = epilogue(acc, *[e[rows_of(tm), :] if e.shape[0] == tm else e[...] for e in extras])
        for o, r in zip(outs, res):
            if len(o.shape) == 2:
                o[rows_of(tm), :] = r.astype(o.dtype)
            else:
                o[0, rows_of(o.shape[1]), :] = r.astype(o.dtype)


def _matmul(a, w, *, col0, n_cols, tm, tn, epilogue, extras=(), extra_specs=(), out_shapes, out_specs, name,
            row_parts=1):
    M, K = a.shape
    cb = col0 // tn
    return pl.pallas_call(
        functools.partial(_mm_kernel, n_extra=len(extras), epilogue=epilogue, row_parts=row_parts),
        grid=(n_cols // tn, M // tm),
        in_specs=[pl.BlockSpec((tm, K), lambda j, i: (i, 0)),
                  pl.BlockSpec((K, tn), lambda j, i: (0, cb + j))] + list(extra_specs),
        out_specs=out_specs,
        out_shape=out_shapes,
        compiler_params=_cparams("parallel", "parallel"),
        name=name,
    )(a, w, *extras)


def _qk_norm_rope(acc, gain, cos, sin):
    outs = []
    for h in range(acc.shape[1] // HEAD):
        z = acc[:, h * HEAD:(h + 1) * HEAD]
        y = z * lax.rsqrt(jnp.mean(z * z, axis=-1, keepdims=True) + EPS) * gain
        outs.append(y * cos + pltpu.roll(y, HEAD // 2, 1) * sin)
    return jnp.concatenate(outs, axis=1)


def _q_epilogue(acc, gain, cos, sin, *, q_scale):
    return [_qk_norm_rope(acc, gain, cos, sin) * q_scale]


def _k_epilogue(acc, gain, cos, sin, *, with_mean):
    k = _qk_norm_rope(acc, gain, cos, sin)
    if not with_mean:
        return [k, k]
    nblk = k.shape[0] // MOBA_BLOCK
    means = [jnp.mean(k[n * MOBA_BLOCK:(n + 1) * MOBA_BLOCK], axis=0, keepdims=True) for n in range(nblk)]
    return [k, k, jnp.concatenate(means, axis=0)]


def _rope_tables(pos):
    half = HEAD // 2
    inv_freq = jnp.power(ROPE_THETA, -jnp.arange(half, dtype=F32) * (2.0 / HEAD))
    ang = pos.astype(F32)[:, None] * inv_freq[None, :]
    c, s = jnp.cos(ang), jnp.sin(ang)
    return jnp.concatenate([c, c], axis=1), jnp.concatenate([-s, s], axis=1)


def _in_proj(h, w_in, q_gain, k_gain, cos, sin, *, tm, rope_tiles, with_mean, q_dtype, q_scale, row_parts=1):
    M = h.shape[0]
    DA = N_HEADS * HEAD
    row_tile = lambda j, i: (i, j)
    rope_spec = pl.BlockSpec((tm, HEAD), lambda j, i: (i % rope_tiles, 0))
    gain_spec = pl.BlockSpec((1, HEAD), lambda j, i: (0, 0))
    tile = pl.BlockSpec((tm, DA), row_tile)
    (q,) = _matmul(h, w_in, col0=0, n_cols=DA, tm=tm, tn=DA,
                   epilogue=functools.partial(_q_epilogue, q_scale=q_scale),
                   extras=(q_gain.reshape(1, HEAD), cos, sin), extra_specs=(gain_spec, rope_spec, rope_spec),
                   out_shapes=[jax.ShapeDtypeStruct((M, DA), q_dtype)], out_specs=[tile], name="proj_q",
                   row_parts=row_parts)
    k_shapes = [jax.ShapeDtypeStruct((M, DA), F32), jax.ShapeDtypeStruct((M, DA), BF16)]
    k_specs = [tile, tile]
    if with_mean:
        nb = tm // MOBA_BLOCK
        k_shapes.append(jax.ShapeDtypeStruct((M // tm, nb, DA), F32))
        k_specs.append(pl.BlockSpec((1, nb, DA), lambda j, i: (i, 0, j)))
    k_out = _matmul(h, w_in, col0=DA, n_cols=DA, tm=tm, tn=DA,
                    epilogue=functools.partial(_k_epilogue, with_mean=with_mean),
                    extras=(k_gain.reshape(1, HEAD), cos, sin), extra_specs=(gain_spec, rope_spec, rope_spec),
                    out_shapes=k_shapes, out_specs=k_specs, name="proj_k", row_parts=row_parts)
    v, v_bf = _matmul(h, w_in, col0=2 * DA, n_cols=DA, tm=tm, tn=DA, epilogue=lambda acc: [acc, acc],
                      out_shapes=[jax.ShapeDtypeStruct((M, DA), F32), jax.ShapeDtypeStruct((M, DA), BF16)],
                      out_specs=[tile, tile], name="proj_v")
    (hg,) = _matmul(h, w_in, col0=3 * DA, n_cols=4 * DA, tm=tm, tn=DA, epilogue=lambda acc: [acc],
                    out_shapes=[jax.ShapeDtypeStruct((M, 4 * DA), F32)], out_specs=[tile], name="proj_hgrn")
    (gates,) = _matmul(h, w_in, col0=7 * DA, n_cols=4 * DA, tm=tm, tn=DA, epilogue=lambda acc: [_sigmoid(acc)],
                       out_shapes=[jax.ShapeDtypeStruct((M, 4 * DA), F32)], out_specs=[tile], name="proj_gates",
                       row_parts=row_parts)
    return q, k_out, v, v_bf, hg, gates


MOBA_STEP_BLOCKS = 2
MOBA_TRIP_STEPS = 4


def _page_block_means(pages, o_ref):
    ppb = len(pages) // o_ref.shape[1]
    rows = pages[0].shape[1]
    for b in range(o_ref.shape[1]):
        s = jnp.sum(pages[b * ppb][0], axis=0)
        for p in range(1, ppb):
            s = s + jnp.sum(pages[b * ppb + p][0], axis=0)
        o_ref[0, b] = s * (1.0 / (ppb * rows))


def _extend_q(q, km, first_future):
    B = q.shape[0]
    nb = km.shape[0]
    km_hi = km.astype(BF16)
    km_lo = (km - km_hi.astype(F32)).astype(BF16)
    g = _dot_nt(km_hi, q) + _dot_nt(km_lo, q)
    blk = lax.broadcasted_iota(jnp.int32, (nb, B), 0)
    g = jnp.where(blk < first_future, g, NEG_INF)
    sel = jnp.zeros((nb, B), F32)
    for _ in range(MOBA_TOPK):
        mx = jnp.max(g, axis=0, keepdims=True)
        idx = jnp.min(jnp.where(g == mx, blk, nb), axis=0, keepdims=True)
        hit = blk == idx
        sel = jnp.where(hit, 1.0, sel)
        g = jnp.where(hit, -jnp.inf, g)
    sel = jnp.where(blk < first_future, sel, 0.0).astype(BF16)
    eye = (lax.broadcasted_iota(jnp.int32, (nb, HEAD), 0) == lax.broadcasted_iota(jnp.int32, (nb, HEAD), 1))
    sel_rows = _dot_tn(sel, jnp.where(eye, 1.0, 0.0).astype(BF16))
    return jnp.concatenate([q, jnp.where(sel_rows > 0.5, 0.0, NEG_INF).astype(BF16)], axis=1)


def _moba_prompt_kernel(pt_ref, q_ref, k_ref, v_ref, km_ref, ind_ref, *rest):
    *pages, o_ref, pool_mean_ref, s_a, s_b = rest
    cur = pl.program_id(2)
    B = MOBA_BLOCK
    nb = km_ref.shape[1]
    _page_block_means(pages, pool_mean_ref)
    q = q_ref[0]
    lane = lax.broadcasted_iota(jnp.int32, (1, HEAD), 1)
    ones_col = jnp.where(lane == 0, 1.0, 0.0).astype(BF16)

    def with_ones(v):
        return jnp.concatenate([v, jnp.broadcast_to(ones_col, v.shape)], axis=1)

    own = pl.multiple_of(cur * B, B)
    s = _dot_nt(q, k_ref[0, pl.ds(own, B), :])
    row = lax.broadcasted_iota(jnp.int32, (B, B), 0)
    col = lax.broadcasted_iota(jnp.int32, (B, B), 1)
    s = jnp.where(col <= row, s, NEG_INF)
    m = jnp.max(s, axis=-1, keepdims=True)
    acc = _dot(jnp.exp2(s - m).astype(BF16), with_ones(v_ref[0, pl.ds(own, B), :]))

    q_aug = _extend_q(q, km_ref[0], cur)

    W = MOBA_STEP_BLOCKS * B
    T = MOBA_TRIP_STEPS
    last_step = nb // MOBA_STEP_BLOCKS - 1
    n_trips = (cur + T * MOBA_STEP_BLOCKS - 1) // (T * MOBA_STEP_BLOCKS)

    def scores(n):
        start = pl.multiple_of(n * W, W)
        k_aug = jnp.concatenate([k_ref[0, pl.ds(start, W), :], ind_ref[pl.ds(start, W), :]], axis=1)
        return _dot_nt(q_aug, k_aug)

    def softmax_step(s_ref, n, m, acc):
        s = s_ref[...]
        m_new = jnp.maximum(m, jnp.max(s, axis=-1, keepdims=True))
        p = jnp.exp2(s - m_new).astype(BF16)
        start = pl.multiple_of(n * W, W)
        return m_new, jnp.exp2(m - m_new) * acc + _dot(p, with_ones(v_ref[0, pl.ds(start, W), :]))

    def body(j, carry):
        m, acc = carry
        bufs = (s_a, s_b)
        for t in range(T):
            bufs[(t + 1) % 2][...] = scores(jnp.minimum(T * j + t + 1, last_step))
            m, acc = softmax_step(bufs[t % 2], T * j + t, m, acc)
        return m, acc

    s_a[...] = scores(0)
    m, acc = lax.fori_loop(0, n_trips, body, (m, acc))
    o_ref[0] = (acc[:, :HEAD] / acc[:, HEAD:HEAD + 1]).astype(o_ref.dtype)


def _moba_prompt(q, k, v, kmean, pool, page_table, ppb):
    Bn, S, _ = q.shape
    nb = S // MOBA_BLOCK
    assert nb % (MOBA_TRIP_STEPS * MOBA_STEP_BLOCKS) == 0 and MOBA_TRIP_STEPS % 2 == 0 and nb <= HEAD
    Bd, n_pages = page_table.shape
    _, page, H, hd = pool.shape
    n_steps = Bn * N_HEADS * nb
    P = Bd * n_pages // n_steps
    assert P * n_steps == Bd * n_pages and P % ppb == 0 and n_pages % P == 0
    G = n_pages // P

    def page_map(b, h, i, pt, *, p):
        t = (b * N_HEADS + h) * nb + i
        return (pt[t // G, (t % G) * P + p], 0, 0, 0)

    def mean_map(b, h, i, pt):
        t = (b * N_HEADS + h) * nb + i
        return (t // G, t % G, 0, 0)

    s_buf = pltpu.VMEM((MOBA_BLOCK, MOBA_STEP_BLOCKS * MOBA_BLOCK), F32)
    ind = (jnp.arange(S)[:, None] // MOBA_BLOCK == jnp.arange(HEAD)[None, :]).astype(BF16)
    return pl.pallas_call(
        _moba_prompt_kernel,
        grid_spec=pltpu.PrefetchScalarGridSpec(
            num_scalar_prefetch=1,
            grid=(Bn, N_HEADS, nb),
            in_specs=[pl.BlockSpec((1, MOBA_BLOCK, HEAD), lambda b, h, i, pt: (b, i, h)),
                      pl.BlockSpec((1, S, HEAD), lambda b, h, i, pt: (b, 0, h)),
                      pl.BlockSpec((1, S, HEAD), lambda b, h, i, pt: (b, 0, h)),
                      pl.BlockSpec((1, nb, HEAD), lambda b, h, i, pt: (b, 0, h)),
                      pl.BlockSpec((S, HEAD), lambda b, h, i, pt: (0, 0))]
                     + [pl.BlockSpec((1, page, H, hd), functools.partial(page_map, p=p)) for p in range(P)],
            out_specs=[pl.BlockSpec((1, MOBA_BLOCK, HEAD), lambda b, h, i, pt: (b, i, h)),
                       pl.BlockSpec((1, P // ppb, H, hd), mean_map)],
            scratch_shapes=[s_buf, s_buf]),
        out_shape=[jax.ShapeDtypeStruct(q.shape, BF16), jax.ShapeDtypeStruct((Bd, n_pages // ppb, H, hd), F32)],
        compiler_params=_cparams("arbitrary", "arbitrary", "arbitrary"),
        name="moba_prompt",
    )(page_table, q, k, v, kmean, ind, *([pool] * P))


HGRN_HEAD_GROUP = 4


def _hgrn_chunk(q_raw, z, v, gate, lb, gain, st, *, blk, valid):
    C = z.shape[0]
    logf = jnp.log(lb + (1.0 - lb) * _sigmoid(z))
    kk = (1.0 - lb) * _sigmoid(-z)
    rowi = lax.broadcasted_iota(jnp.int32, (C, 1), 0)
    if valid < C:
        logf = jnp.where(rowi < valid, logf, 0.0)
        kk = jnp.where(rowi < valid, kk, 0.0)
    q = _silu(q_raw)
    v = v.astype(BF16)

    tri = lax.broadcasted_iota(jnp.int32, (C, C), 0) >= lax.broadcasted_iota(jnp.int32, (C, C), 1)
    tri_bf = jnp.where(tri, 1.0, 0.0).astype(BF16)
    g1 = logf.astype(BF16)
    r1 = logf - g1.astype(F32)
    g2 = r1.astype(BF16)
    g3 = (r1 - g2.astype(F32)).astype(BF16)
    A = _dot(tri_bf, g1) + _dot(tri_bf, g2) + _dot(tri_bf, g3)
    a_last = A[C - 1:C, :]

    o_inter = _dot_nt((q * jnp.exp(A)).astype(BF16), st.astype(BF16))
    k_dec = (kk * jnp.exp(a_last - A)).astype(BF16)
    st_new = st * jnp.exp(a_last) + _dot_tn(v, k_dec)

    pieces = []
    for j in range(C // blk):
        lo, hi = j * blk, (j + 1) * blk
        a_ref_row = A[lo + blk // 2 - 1:lo + blk // 2, :]
        qt = (q[lo:hi] * jnp.exp(A[lo:hi] - a_ref_row)).astype(BF16)
        kt = (kk * jnp.exp(jnp.where(rowi < hi, a_ref_row - A, NEG_INF))).astype(BF16)
        pieces.append(_dot_nt(qt, kt))
    scores = pieces[0] if len(pieces) == 1 else jnp.concatenate(pieces, axis=0)
    scores = jnp.where(tri, scores, 0.0).astype(BF16)
    o = _dot(scores, v) + o_inter

    y = o * lax.rsqrt(jnp.mean(o * o, axis=-1, keepdims=True) + EPS) * gain
    return y * _silu(gate), st_new


def _hgrn_kernel(q_ref, z_ref, v_ref, g_ref, lb_ref, gn_ref, s0_ref, o_ref, s_ref, st_scr, *, blk):
    n_heads = s0_ref.shape[1]

    @pl.when(pl.program_id(2) == 0)
    def _():
        for h in range(n_heads):
            st_scr[h] = s0_ref[0, h].T

    for h in range(n_heads):
        cols = slice(h * HEAD, (h + 1) * HEAD)
        o, st_new = _hgrn_chunk(q_ref[0, :, cols], z_ref[0, :, cols], v_ref[0, :, cols], g_ref[0, :, cols],
                                lb_ref[:, cols], gn_ref[...], st_scr[h], blk=blk, valid=q_ref.shape[1])
        st_scr[h] = st_new
        s_ref[0, h] = st_new.T
        o_ref[0, :, cols] = o.astype(o_ref.dtype)


def _hgrn_decode_kernel(hg_ref, lb_ref, gn_ref, s0_ref, o_ref, s_ref, *, valid):
    D = N_HEADS * HEAD
    for h in range(N_HEADS):
        cols = [hg_ref[0, :, g * D + h * HEAD:g * D + (h + 1) * HEAD] for g in range(4)]
        o, st_new = _hgrn_chunk(*cols, lb_ref[:, h * HEAD:(h + 1) * HEAD], gn_ref[...], s0_ref[0, h].T,
                                blk=hg_ref.shape[1], valid=valid)
        s_ref[0, h] = st_new.T
        o_ref[0, :, h * HEAD:(h + 1) * HEAD] = o.astype(o_ref.dtype)


def _hgrn_decode(hg, lb, g_norm, s0, *, valid):
    Bn, R, _ = hg.shape
    D = N_HEADS * HEAD
    state = pl.BlockSpec((1, N_HEADS, HEAD, HEAD), lambda b: (b, 0, 0, 0))
    return pl.pallas_call(
        functools.partial(_hgrn_decode_kernel, valid=valid),
        grid=(Bn,),
        in_specs=[pl.BlockSpec((1, R, 4 * D), lambda b: (b, 0, 0)), pl.BlockSpec((1, D), lambda b: (0, 0)),
                  pl.BlockSpec((1, HEAD), lambda b: (0, 0)), state],
        out_specs=[pl.BlockSpec((1, R, D), lambda b: (b, 0, 0)), state],
        out_shape=[jax.ShapeDtypeStruct((Bn, R, D), BF16), jax.ShapeDtypeStruct(s0.shape, F32)],
        compiler_params=_cparams("parallel"),
        name="hgrn_decode",
    )(hg, lb.reshape(1, D), g_norm.reshape(1, HEAD), s0)


def _hgrn(hg, lb, g_norm, s0, *, chunk, blk):
    Bn, T, _ = hg.shape
    H = N_HEADS
    HG = HGRN_HEAD_GROUP
    W = HG * HEAD
    col = lambda g: pl.BlockSpec((1, chunk, W), lambda b, h, c: (b, c, g * (H // HG) + h))
    vec = pl.BlockSpec((1, W), lambda b, h, c: (0, h))
    state = pl.BlockSpec((1, HG, HEAD, HEAD), lambda b, h, c: (b, h, 0, 0))
    return pl.pallas_call(
        functools.partial(_hgrn_kernel, blk=blk),
        grid=(Bn, H // HG, T // chunk),
        in_specs=[col(0), col(1), col(2), col(3), vec, pl.BlockSpec((1, HEAD), lambda b, h, c: (0, 0)), state],
        out_specs=[pl.BlockSpec((1, chunk, W), lambda b, h, c: (b, c, h)), state],
        out_shape=[jax.ShapeDtypeStruct((Bn, T, H * HEAD), BF16), jax.ShapeDtypeStruct(s0.shape, F32)],
        scratch_shapes=[pltpu.VMEM((HG, HEAD, HEAD), F32)],
        compiler_params=_cparams("parallel", "parallel", "arbitrary"),
        name="hgrn",
    )(hg, hg, hg, hg, lb.reshape(1, H * HEAD), g_norm.reshape(1, HEAD), s0)


def _merge_kernel(oa_ref, ob_ref, gates_ref, x_ref, wa_ref, wb_ref, wo_ref, gain_ref, x1_ref, h2_ref):
    D = x_ref.shape[1]
    mix = gates_ref[:, :D] * _dot(oa_ref[...], wa_ref[...]) + gates_ref[:, D:] * _dot(ob_ref[...], wb_ref[...])
    x1 = x_ref[...] + _dot(mix.astype(BF16), wo_ref[...])
    x1_ref[...] = x1
    h2_ref[...] = (x1 * lax.rsqrt(jnp.mean(x1 * x1, axis=-1, keepdims=True) + EPS) * gain_ref[...]).astype(h2_ref.dtype)


def _resident(shape):
    return pl.BlockSpec(shape, lambda *_: (0,) * len(shape), pipeline_mode=pl.Buffered(1))


def _merge_out(o_a, o_b, gates, x, w_ba, w_bb, w_out, norm_ffn, *, tm):
    M, D = x.shape
    DA = o_a.shape[1]
    rows = lambda w: pl.BlockSpec((tm, w), lambda i: (i, 0))
    return pl.pallas_call(
        _merge_kernel,
        grid=(M // tm,),
        in_specs=[rows(DA), rows(DA), rows(2 * D), rows(D), _resident((DA, D)), _resident((DA, D)), _resident((D, D)),
                  _resident((1, D))],
        out_specs=[rows(D), rows(D)],
        out_shape=[jax.ShapeDtypeStruct((M, D), F32), jax.ShapeDtypeStruct((M, D), BF16)],
        compiler_params=_cparams("parallel"),
        name="merge_out",
    )(o_a, o_b, gates, x, w_ba, w_bb, w_out, norm_ffn.reshape(1, D))


def _ffn_kernel(h_ref, wg_ref, wu_ref, wd_ref, cw_ref, cb_ref, x_ref, e1_ref, e2_ref,
                y_ref, g_ref, tail_scr, *, tiles_per_seq, period):
    i, f = pl.program_id(0), pl.program_id(1)
    tm = h_ref.shape[0]

    @pl.when(f == 0)
    def _():
        y_ref[...] = x_ref[...]

    if period is None:
        @pl.when(i % tiles_per_seq == 0)
        def _():
            tail_scr[f, 0:1, :] = e2_ref[0]
            tail_scr[f, 1:2, :] = e1_ref[0]

    h = h_ref[...]
    g = _dot(h, wg_ref[...])
    u = _dot(h, wu_ref[...])
    rowi = lax.broadcasted_iota(jnp.int32, (tm, 1), 0)
    r1 = pltpu.roll(g, 1, 0)
    r2 = pltpu.roll(g, 2, 0)
    if period is None:
        p2 = tail_scr[f, 0:1, :]
        p1 = tail_scr[f, 1:2, :]
        g1 = jnp.where(rowi == 0, p1, r1)
        g2 = jnp.where(rowi == 0, p2, jnp.where(rowi == 1, p1, r2))
        tail_scr[f, 0:2, :] = g[tm - 2:tm]
    else:
        t = rowi % period
        g1 = jnp.where(t == 0, e1_ref[...], r1)
        g2 = jnp.where(t < 2, e2_ref[...], r2)
    g_ref[...] = g.reshape(g_ref.shape) if period is not None else g[tm - 8:tm].reshape(g_ref.shape)
    cw = cw_ref[...]
    c = cb_ref[...] + cw[0:1] * g2 + cw[1:2] * g1 + cw[2:3] * g
    y_ref[...] += _dot((_silu(c) * u).astype(BF16), wd_ref[...])


def _ffn(h2, x1, w_gate, w_up, w_down, conv_w, conv_b, e1, e2, *, tm, tf, rows_per_seq, period):
    M, D = x1.shape
    F = w_gate.shape[1]
    nf = F // tf
    if period is None:
        tps = rows_per_seq // tm
        e_spec = pl.BlockSpec((1, 1, tf), lambda i, f: (i // tps, 0, f))
        g_shape = jax.ShapeDtypeStruct((M // tm, 8, F), F32)
        g_spec = pl.BlockSpec((1, 8, tf), lambda i, f: (i, 0, f))
    else:
        tps = 1
        e_spec = pl.BlockSpec((tm, tf), lambda i, f: (i, f))
        g_shape = jax.ShapeDtypeStruct((M, F), F32)
        g_spec = pl.BlockSpec((tm, tf), lambda i, f: (i, f))
    return pl.pallas_call(
        functools.partial(_ffn_kernel, tiles_per_seq=tps, period=period),
        grid=(M // tm, nf),
        in_specs=[pl.BlockSpec((tm, D), lambda i, f: (i, 0)),
                  pl.BlockSpec((D, tf), lambda i, f: (0, f)),
                  pl.BlockSpec((D, tf), lambda i, f: (0, f)),
                  pl.BlockSpec((tf, D), lambda i, f: (f, 0)),
                  pl.BlockSpec((CONV_W, tf), lambda i, f: (0, f)),
                  pl.BlockSpec((1, tf), lambda i, f: (0, f)),
                  pl.BlockSpec((tm, D), lambda i, f: (i, 0)),
                  e_spec, e_spec],
        out_specs=[pl.BlockSpec((tm, D), lambda i, f: (i, 0)), g_spec],
        out_shape=[jax.ShapeDtypeStruct((M, D), F32), g_shape],
        scratch_shapes=[pltpu.VMEM((nf, 8, tf), F32)],
        compiler_params=_cparams("arbitrary", "arbitrary"),
        name="ffn",
    )(h2, w_gate, w_up, w_down, conv_w, conv_b.reshape(1, F), x1, e1, e2)


def _ple_kernel(x_ref, gain_ref, wg_ref, p_ref, wp_ref, o_ref):
    x = x_ref[...]
    hn = (x * lax.rsqrt(jnp.mean(x * x, axis=-1, keepdims=True) + EPS) * gain_ref[...]).astype(BF16)
    gate = _sigmoid(_dot(hn, wg_ref[...]))
    o_ref[...] = x + gate * _dot(p_ref[...].astype(BF16), wp_ref[...])


def _ple(x2, p, norm_ple, w_gate, w_proj, *, tm):
    M, D = x2.shape
    P = p.shape[1]
    return pl.pallas_call(
        _ple_kernel,
        grid=(M // tm,),
        in_specs=[pl.BlockSpec((tm, D), lambda i: (i, 0)), _resident((1, D)), _resident((D, D)),
                  pl.BlockSpec((tm, P), lambda i: (i, 0)), _resident((P, D))],
        out_specs=pl.BlockSpec((tm, D), lambda i: (i, 0)),
        out_shape=jax.ShapeDtypeStruct((M, D), F32),
        compiler_params=_cparams("parallel"),
        name="ple",
    )(x2, norm_ple.reshape(1, D), w_gate, p, w_proj)


def _decode_select_kernel(q_ref, km_ref, o_ref):
    nb = km_ref.shape[2]
    for h in range(N_HEADS):
        q = q_ref[0, :, h * HEAD:(h + 1) * HEAD].astype(BF16)
        km = km_ref[0, h]
        km_hi = km.astype(BF16)
        km_lo = (km - km_hi.astype(F32)).astype(BF16)
        g = _dot_nt(q, km_hi) + _dot_nt(q, km_lo)
        blk = lax.broadcasted_iota(jnp.int32, g.shape, 1)
        lane = lax.broadcasted_iota(jnp.int32, (g.shape[0], 128), 1)
        out = jnp.zeros((g.shape[0], 128), jnp.int32)
        for k in range(MOBA_TOPK):
            mx = jnp.max(g, axis=-1, keepdims=True)
            idx = jnp.min(jnp.where(g == mx, blk, nb), axis=-1, keepdims=True)
            out = jnp.where(lane == k, idx, out)
            g = jnp.where(blk == idx, -jnp.inf, g)
        o_ref[0, h] = out


def _decode_select(q_pad, kmean):
    Bd, R, _ = q_pad.shape
    nb = kmean.shape[2]
    return pl.pallas_call(
        _decode_select_kernel,
        grid=(Bd,),
        in_specs=[pl.BlockSpec((1, R, N_HEADS * HEAD), lambda s: (s, 0, 0)),
                  pl.BlockSpec((1, N_HEADS, nb, HEAD), lambda s: (s, 0, 0, 0))],
        out_specs=pl.BlockSpec((1, N_HEADS, R, 128), lambda s: (s, 0, 0, 0)),
        out_shape=jax.ShapeDtypeStruct((Bd, N_HEADS, R, 128), jnp.int32),
        compiler_params=_cparams("parallel"),
        name="decode_select",
    )(q_pad, kmean)


def _decode_attn_kernel(pg_ref, q_ref, kn_ref, vn_ref, pk_ref, pv_ref, o_ref, kbuf, vbuf, sem, *, n_tok, n_slab, scale):
    nh = pl.num_programs(1)
    step = pl.program_id(0) * nh + pl.program_id(1)
    n_steps = pl.num_programs(0) * nh
    slot = step % 2
    per_step = n_tok * n_slab

    def slab_copies(st, sl):
        head = st % nh
        copies = []
        for j in range(per_step):
            pg = pg_ref[st * per_step + j]
            copies.append(pltpu.make_async_copy(pk_ref.at[pg, :, head, :], kbuf.at[sl, j], sem.at[0, sl]))
            copies.append(pltpu.make_async_copy(pv_ref.at[pg, :, head, :], vbuf.at[sl, j], sem.at[1, sl]))
        return copies

    @pl.when(step == 0)
    def _():
        for c in slab_copies(step, slot):
            c.start()

    @pl.when(step + 1 < n_steps)
    def _():
        for c in slab_copies(step + 1, 1 - slot):
            c.start()

    for c in slab_copies(step, slot):
        c.wait()

    R = q_ref.shape[1]
    page = kbuf.shape[2]
    kn = kn_ref[0]
    vn = vn_ref[0]
    jrow = lax.broadcasted_iota(jnp.int32, (R, 1), 0)
    o_ref[...] = jnp.zeros(o_ref.shape, o_ref.dtype)
    for t in range(n_tok):
        ks = kbuf[slot, t * n_slab:(t + 1) * n_slab].reshape(n_slab * page, HEAD).astype(BF16)
        vs = vbuf[slot, t * n_slab:(t + 1) * n_slab].reshape(n_slab * page, HEAD).astype(BF16)
        qrow = q_ref[0, t:t + 1, :]
        q8 = jnp.broadcast_to(qrow, (R, HEAD)).astype(BF16)
        s_sel = _dot_nt(q8, ks)[0:1] * scale
        s_own = jnp.sum(kn * qrow, axis=-1, keepdims=True) * scale
        s_own = jnp.where(jrow <= t, s_own, NEG_INF)
        m = jnp.maximum(jnp.max(s_sel, axis=-1, keepdims=True), jnp.max(s_own, axis=0, keepdims=True))
        p_sel = jnp.exp(s_sel - m)
        p_own = jnp.exp(s_own - m)
        l = jnp.sum(p_sel, axis=-1, keepdims=True) + jnp.sum(p_own, axis=0, keepdims=True)
        o = _dot(jnp.broadcast_to(p_sel, (R, p_sel.shape[1])).astype(BF16), vs)[0:1]
        o = o + jnp.sum(p_own * vn, axis=0, keepdims=True)
        o_ref[0, t:t + 1, :] = o / l


def _decode_attn(q_pad, kn_pad, vn_pad, pool_k, pool_v, sel_pages, *, n_tok):
    Bd, R, _ = q_pad.shape
    page = pool_k.shape[1]
    n_slab = sel_pages.shape[0] // (Bd * N_HEADS * n_tok)
    row = pl.BlockSpec((1, R, HEAD), lambda s, h, pg: (s, 0, h))
    hbm = pl.BlockSpec(memory_space=pl.ANY)
    slabs = pltpu.VMEM((2, n_tok * n_slab, page, HEAD), F32)
    return pl.pallas_call(
        functools.partial(_decode_attn_kernel, n_tok=n_tok, n_slab=n_slab, scale=HEAD ** -0.5),
        grid_spec=pltpu.PrefetchScalarGridSpec(
            num_scalar_prefetch=1, grid=(Bd, N_HEADS), in_specs=[row, row, row, hbm, hbm], out_specs=row,
            scratch_shapes=[slabs, slabs, pltpu.SemaphoreType.DMA((2, 2))]),
        out_shape=jax.ShapeDtypeStruct(q_pad.shape, F32),
        compiler_params=_cparams("arbitrary", "arbitrary"),
        name="decode_attn",
    )(sel_pages, q_pad, kn_pad, vn_pad, pool_k, pool_v)


def _pad_rows(a, rows):
    return jnp.pad(a, ((0, 0), (0, rows - a.shape[1]), (0, 0)))


def kernel(x_prompt, x_sample, cache_k, cache_v, state_hgrn, state_conv, page_table, p_prompt, p_sample,
           norm_mix, w_in, q_norm, k_norm, lb_logits, g_norm_b, w_branch_a, w_branch_b, w_out,
           norm_ffn, w_ffn_gate, w_ffn_up, conv_w, conv_b, w_ffn_down, norm_ple, w_ple_gate, w_ple_proj):
    Bp, S, D = x_prompt.shape
    Bd, T, _ = x_sample.shape
    depth, n_pool, page, H, hd = cache_k.shape
    n_pages = page_table.shape[1]
    F = w_ffn_gate.shape[-1]
    DA = H * hd
    ppb = MOBA_BLOCK // page
    assert depth == 1 and H == N_HEADS and hd == HEAD
    assert (n_pages * page) % MOBA_BLOCK == 0, "past tokens must fill whole attention blocks"
    assert n_pages // ppb >= MOBA_TOPK and S % MOBA_BLOCK == 0 and T <= 8

    bf = lambda w: w[0].astype(BF16)
    w_in_b, w_ba, w_bb, w_o = bf(w_in), bf(w_branch_a), bf(w_branch_b), bf(w_out)
    w_fg, w_fu, w_fd, w_pg, w_pp = bf(w_ffn_gate), bf(w_ffn_up), bf(w_ffn_down), bf(w_ple_gate), bf(w_ple_proj)
    lb = jnp.cumsum(jax.nn.softmax(lb_logits.astype(F32), axis=0), axis=0)[0]

    Mp = Bp * S
    tm = 512
    xp = x_prompt.reshape(Mp, D)
    cos_p, sin_p = _rope_tables(jnp.arange(S))
    h = _rmsnorm(xp, norm_mix[0], tm)
    q, (k, k_bf, kmean), v, v_bf, hg, gates = _in_proj(
        h, w_in_b, q_norm[0], k_norm[0], cos_p, sin_p, tm=tm, rope_tiles=S // tm, with_mean=True, q_dtype=BF16,
        q_scale=HEAD ** -0.5 * LOG2E, row_parts=2)
    pool_k = cache_k.reshape(n_pool, page, H, hd)
    o_a, kmean_s = _moba_prompt(q.reshape(Bp, S, DA), k_bf.reshape(Bp, S, DA), v_bf.reshape(Bp, S, DA),
                                kmean.reshape(Bp, S // MOBA_BLOCK, DA), pool_k, page_table, ppb)
    o_b, hgrn_p = _hgrn(hg.reshape(Bp, S, 4 * DA), lb, g_norm_b[0], jnp.zeros((Bp, H, hd, hd), F32),
                        chunk=256, blk=32)
    x1, h2 = _merge_out(o_a.reshape(Mp, DA), o_b.reshape(Mp, DA), gates, xp, w_ba, w_bb, w_o, norm_ffn[0], tm=256)
    zeros_e = jnp.zeros((Bp, 1, F), F32)
    x2, g_tail = _ffn(h2, x1, w_fg, w_fu, w_fd, conv_w[0], conv_b[0], zeros_e, zeros_e,
                      tm=512, tf=512, rows_per_seq=S, period=None)
    y_prompt = _ple(x2, p_prompt[0].reshape(Mp, -1), norm_ple[0], w_pg, w_pp, tm=256).reshape(Bp, S, D)
    k_prompt = k.reshape(1, Bp, S, H, hd)
    v_prompt = v.reshape(1, Bp, S, H, hd)
    tps = S // 512
    conv_prompt = g_tail[tps - 1::tps, 8 - (CONV_W - 1):, :][None]

    Ms = Bd * T
    R = 8
    xs = x_sample.reshape(Ms, D)
    cos_s, sin_s = _rope_tables(n_pages * page + jnp.arange(T))
    cos_s, sin_s = jnp.tile(cos_s, (Bd, 1)), jnp.tile(sin_s, (Bd, 1))
    hs = _rmsnorm(xs, norm_mix[0], Ms)
    qs, (ks, _), vs, _, hgs, gates_s = _in_proj(
        hs, w_in_b, q_norm[0], k_norm[0], cos_s, sin_s, tm=Ms, rope_tiles=1, with_mean=False, q_dtype=F32, q_scale=1.0)
    q_pad = _pad_rows(qs.reshape(Bd, T, DA), R)
    sel = _decode_select(q_pad, jnp.transpose(kmean_s, (0, 2, 1, 3)))[:, :, :T, :MOBA_TOPK]
    blk_pages = page_table.reshape(Bd, n_pages // ppb, ppb)
    sel_pages = blk_pages[jnp.arange(Bd)[:, None, None, None], sel]
    o_as = _decode_attn(q_pad, _pad_rows(ks.reshape(Bd, T, DA), R), _pad_rows(vs.reshape(Bd, T, DA), R),
                        pool_k, cache_v.reshape(n_pool, page, H, hd),
                        sel_pages.reshape(-1).astype(jnp.int32), n_tok=T)[:, :T]
    RH = 16
    o_bs, hgrn_s = _hgrn_decode(_pad_rows(hgs.reshape(Bd, T, 4 * DA), RH), lb, g_norm_b[0], state_hgrn[0], valid=T)
    x1s, h2s = _merge_out(o_as.reshape(Ms, DA).astype(BF16), o_bs[:, :T].reshape(Ms, DA), gates_s, xs,
                          w_ba, w_bb, w_o, norm_ffn[0], tm=Ms)
    buf = state_conv[0]
    zrow = jnp.zeros((Bd, 1, F), F32)
    e1 = jnp.concatenate([buf[:, 1:2], jnp.tile(zrow, (1, T - 1, 1))], axis=1).reshape(Ms, F)
    e2 = jnp.concatenate([buf[:, 0:1], buf[:, 1:2], jnp.tile(zrow, (1, T - 2, 1))], axis=1).reshape(Ms, F)
    x2s, g_s = _ffn(h2s, x1s, w_fg, w_fu, w_fd, conv_w[0], conv_b[0], e1, e2,
                    tm=Ms, tf=512, rows_per_seq=T, period=T)
    y_sample = _ple(x2s, p_sample[0].reshape(Ms, -1), norm_ple[0], w_pg, w_pp, tm=Ms).reshape(Bd, T, D)
    k_sample = ks.reshape(1, Bd, T, H, hd)
    v_sample = vs.reshape(1, Bd, T, H, hd)
    conv_sample = g_s.reshape(Bd, T, F)[:, T - (CONV_W - 1):][None]

    return (y_prompt, y_sample, k_prompt, v_prompt, hgrn_p[None], conv_prompt,
            k_sample, v_sample, hgrn_s[None], conv_sample)
```

```python
import functools

import jax
import jax.numpy as jnp
from jax import lax
from jax.experimental import pallas as pl
from jax.experimental.pallas import tpu as pltpu

F32 = jnp.float32
BF16 = jnp.bfloat16
EPS = 1e-6
NEG_INF = -1e30
ROPE_THETA = 10000.0
LOG2E = 1.4426950408889634

HEAD = 128
N_HEADS = 8
MOBA_BLOCK = 256
MOBA_TOPK = 3
CONV_W = 3

VMEM_LIMIT = 52 * 1024 * 1024


def _cparams(*sem):
    return pltpu.CompilerParams(dimension_semantics=sem, vmem_limit_bytes=VMEM_LIMIT)


def _dot(a, b):
    return jnp.dot(a, b, preferred_element_type=F32)


def _dot_nt(a, b):
    return lax.dot_general(a, b, (((1,), (1,)), ((), ())), preferred_element_type=F32)


def _dot_tn(a, b):
    return lax.dot_general(a, b, (((0,), (0,)), ((), ())), preferred_element_type=F32)


def _sigmoid(x):
    return 1.0 / (1.0 + jnp.exp(-x))


def _silu(x):
    return x * _sigmoid(x)


def _rmsnorm_kernel(x_ref, g_ref, o_ref):
    x = x_ref[...]
    ms = jnp.mean(x * x, axis=-1, keepdims=True)
    o_ref[...] = (x * lax.rsqrt(ms + EPS) * g_ref[...]).astype(o_ref.dtype)


def _rmsnorm(x, gain, tm):
    M, D = x.shape
    return pl.pallas_call(
        _rmsnorm_kernel,
        grid=(M // tm,),
        in_specs=[pl.BlockSpec((tm, D), lambda i: (i, 0)), pl.BlockSpec((1, D), lambda i: (0, 0))],
        out_specs=pl.BlockSpec((tm, D), lambda i: (i, 0)),
        out_shape=jax.ShapeDtypeStruct((M, D), BF16),
        compiler_params=_cparams("parallel"),
        name="rmsnorm",
    )(x, gain.reshape(1, D))


def _mm_kernel(*refs, n_extra, epilogue, row_parts):
    a_ref, w_ref = refs[:2]
    extras = refs[2:2 + n_extra]
    outs = refs[2 + n_extra:]
    tm = a_ref.shape[0]
    for part in range(row_parts):
        def rows_of(n_rows):
            return slice(part * (n_rows // row_parts), (part + 1) * (n_rows // row_parts))
        acc = _dot(a_ref[rows_of(tm), :], w_ref[...])
        res = epilogue(acc, *[e[rows_of(tm), :] if e.shape[0] == tm else e[...] for e in extras])
        for o, r in zip(outs, res):
            if len(o.shape) == 2:
                o[rows_of(tm), :] = r.astype(o.dtype)
            else:
                o[0, rows_of(o.shape[1]), :] = r.astype(o.dtype)


def _matmul(a, w, *, col0, n_cols, tm, tn, epilogue, extras=(), extra_specs=(), out_shapes, out_specs, name,
            row_parts=1):
    M, K = a.shape
    cb = col0 // tn
    return pl.pallas_call(
        functools.partial(_mm_kernel, n_extra=len(extras), epilogue=epilogue, row_parts=row_parts),
        grid=(n_cols // tn, M // tm),
        in_specs=[pl.BlockSpec((tm, K), lambda j, i: (i, 0)),
                  pl.BlockSpec((K, tn), lambda j, i: (0, cb + j))] + list(extra_specs),
        out_specs=out_specs,
        out_shape=out_shapes,
        compiler_params=_cparams("parallel", "parallel"),
        name=name,
    )(a, w, *extras)


def _qk_norm_rope(acc, gain, cos, sin):
    outs = []
    for h in range(acc.shape[1] // HEAD):
        z = acc[:, h * HEAD:(h + 1) * HEAD]
        y = z * lax.rsqrt(jnp.mean(z * z, axis=-1, keepdims=True) + EPS) * gain
        outs.append(y * cos + pltpu.roll(y, HEAD // 2, 1) * sin)
    return jnp.concatenate(outs, axis=1)


def _q_epilogue(acc, gain, cos, sin, *, q_scale):
    return [_qk_norm_rope(acc, gain, cos, sin) * q_scale]


def _k_epilogue(acc, gain, cos, sin, *, with_mean):
    k = _qk_norm_rope(acc, gain, cos, sin)
    if not with_mean:
        return [k, k]
    nblk = k.shape[0] // MOBA_BLOCK
    means = [jnp.mean(k[n * MOBA_BLOCK:(n + 1) * MOBA_BLOCK], axis=0, keepdims=True) for n in range(nblk)]
    return [k, k, jnp.concatenate(means, axis=0)]


def _rope_tables(pos):
    half = HEAD // 2
    inv_freq = jnp.power(ROPE_THETA, -jnp.arange(half, dtype=F32) * (2.0 / HEAD))
    ang = pos.astype(F32)[:, None] * inv_freq[None, :]
    c, s = jnp.cos(ang), jnp.sin(ang)
    return jnp.concatenate([c, c], axis=1), jnp.concatenate([-s, s], axis=1)


def _in_proj(h, w_in, q_gain, k_gain, cos, sin, *, tm, rope_tiles, with_mean, q_dtype, q_scale, row_parts=1):
    M = h.shape[0]
    DA = N_HEADS * HEAD
    row_tile = lambda j, i: (i, j)
    rope_spec = pl.BlockSpec((tm, HEAD), lambda j, i: (i % rope_tiles, 0))
    gain_spec = pl.BlockSpec((1, HEAD), lambda j, i: (0, 0))
    tile = pl.BlockSpec((tm, DA), row_tile)
    (q,) = _matmul(h, w_in, col0=0, n_cols=DA, tm=tm, tn=DA,
                   epilogue=functools.partial(_q_epilogue, q_scale=q_scale),
                   extras=(q_gain.reshape(1, HEAD), cos, sin), extra_specs=(gain_spec, rope_spec, rope_spec),
                   out_shapes=[jax.ShapeDtypeStruct((M, DA), q_dtype)], out_specs=[tile], name="proj_q",
                   row_parts=row_parts)
    k_shapes = [jax.ShapeDtypeStruct((M, DA), F32), jax.ShapeDtypeStruct((M, DA), BF16)]
    k_specs = [tile, tile]
    if with_mean:
        nb = tm // MOBA_BLOCK
        k_shapes.append(jax.ShapeDtypeStruct((M // tm, nb, DA), F32))
        k_specs.append(pl.BlockSpec((1, nb, DA), lambda j, i: (i, 0, j)))
    k_out = _matmul(h, w_in, col0=DA, n_cols=DA, tm=tm, tn=DA,
                    epilogue=functools.partial(_k_epilogue, with_mean=with_mean),
                    extras=(k_gain.reshape(1, HEAD), cos, sin), extra_specs=(gain_spec, rope_spec, rope_spec),
                    out_shapes=k_shapes, out_specs=k_specs, name="proj_k", row_parts=row_parts)
    v, v_bf = _matmul(h, w_in, col0=2 * DA, n_cols=DA, tm=tm, tn=DA, epilogue=lambda acc: [acc, acc],
                      out_shapes=[jax.ShapeDtypeStruct((M, DA), F32), jax.ShapeDtypeStruct((M, DA), BF16)],
                      out_specs=[tile, tile], name="proj_v")
    (hg,) = _matmul(h, w_in, col0=3 * DA, n_cols=4 * DA, tm=tm, tn=DA, epilogue=lambda acc: [acc],
                    out_shapes=[jax.ShapeDtypeStruct((M, 4 * DA), F32)], out_specs=[tile], name="proj_hgrn")
    (gates,) = _matmul(h, w_in, col0=7 * DA, n_cols=4 * DA, tm=tm, tn=DA, epilogue=lambda acc: [_sigmoid(acc)],
                       out_shapes=[jax.ShapeDtypeStruct((M, 4 * DA), F32)], out_specs=[tile], name="proj_gates",
                       row_parts=row_parts)
    return q, k_out, v, v_bf, hg, gates


MOBA_STEP_BLOCKS = 2
MOBA_TRIP_STEPS = 4


def _page_block_means(pages, o_ref):
    ppb = len(pages) // o_ref.shape[1]
    rows = pages[0].shape[1]
    for b in range(o_ref.shape[1]):
        s = jnp.sum(pages[b * ppb][0], axis=0)
        for p in range(1, ppb):
            s = s + jnp.sum(pages[b * ppb + p][0], axis=0)
        o_ref[0, b] = s * (1.0 / (ppb * rows))


def _extend_q(q, km, first_future):
    B = q.shape[0]
    nb = km.shape[0]
    km_hi = km.astype(BF16)
    km_lo = (km - km_hi.astype(F32)).astype(BF16)
    g = _dot_nt(km_hi, q) + _dot_nt(km_lo, q)
    blk = lax.broadcasted_iota(jnp.int32, (nb, B), 0)
    g = jnp.where(blk < first_future, g, NEG_INF)
    sel = jnp.zeros((nb, B), F32)
    for _ in range(MOBA_TOPK):
        mx = jnp.max(g, axis=0, keepdims=True)
        idx = jnp.min(jnp.where(g == mx, blk, nb), axis=0, keepdims=True)
        hit = blk == idx
        sel = jnp.where(hit, 1.0, sel)
        g = jnp.where(hit, -jnp.inf, g)
    sel = jnp.where(blk < first_future, sel, 0.0).astype(BF16)
    eye = (lax.broadcasted_iota(jnp.int32, (nb, HEAD), 0) == lax.broadcasted_iota(jnp.int32, (nb, HEAD), 1))
    sel_rows = _dot_tn(sel, jnp.where(eye, 1.0, 0.0).astype(BF16))
    return jnp.concatenate([q, jnp.where(sel_rows > 0.5, 0.0, NEG_INF).astype(BF16)], axis=1)


def _moba_prompt_kernel(pt_ref, q_ref, k_ref, v_ref, km_ref, ind_ref, *rest):
    *pages, o_ref, pool_mean_ref, s_a, s_b = rest
    cur = pl.program_id(2)
    B = MOBA_BLOCK
    nb = km_ref.shape[1]
    _page_block_means(pages, pool_mean_ref)
    q = q_ref[0]
    lane = lax.broadcasted_iota(jnp.int32, (1, HEAD), 1)
    ones_col = jnp.where(lane == 0, 1.0, 0.0).astype(BF16)

    def with_ones(v):
        return jnp.concatenate([v, jnp.broadcast_to(ones_col, v.shape)], axis=1)

    own = pl.multiple_of(cur * B, B)
    s = _dot_nt(q, k_ref[0, pl.ds(own, B), :])
    row = lax.broadcasted_iota(jnp.int32, (B, B), 0)
    col = lax.broadcasted_iota(jnp.int32, (B, B), 1)
    s = jnp.where(col <= row, s, NEG_INF)
    m = jnp.max(s, axis=-1, keepdims=True)
    acc = _dot(jnp.exp2(s - m).astype(BF16), with_ones(v_ref[0, pl.ds(own, B), :]))

    q_aug = _extend_q(q, km_ref[0], cur)

    W = MOBA_STEP_BLOCKS * B
    T = MOBA_TRIP_STEPS
    last_step = nb // MOBA_STEP_BLOCKS - 1
    n_trips = (cur + T * MOBA_STEP_BLOCKS - 1) // (T * MOBA_STEP_BLOCKS)

    def scores(n):
        start = pl.multiple_of(n * W, W)
        k_aug = jnp.concatenate([k_ref[0, pl.ds(start, W), :], ind_ref[pl.ds(start, W), :]], axis=1)
        return _dot_nt(q_aug, k_aug)

    def softmax_step(s_ref, n, m, acc):
        s = s_ref[...]
        m_new = jnp.maximum(m, jnp.max(s, axis=-1, keepdims=True))
        p = jnp.exp2(s - m_new).astype(BF16)
        start = pl.multiple_of(n * W, W)
        return m_new, jnp.exp2(m - m_new) * acc + _dot(p, with_ones(v_ref[0, pl.ds(start, W), :]))

    def body(j, carry):
        m, acc = carry
        bufs = (s_a, s_b)
        for t in range(T):
            bufs[(t + 1) % 2][...] = scores(jnp.minimum(T * j + t + 1, last_step))
            m, acc = softmax_step(bufs[t % 2], T * j + t, m, acc)
        return m, acc

    s_a[...] = scores(0)
    m, acc = lax.fori_loop(0, n_trips, body, (m, acc))
    o_ref[0] = (acc[:, :HEAD] / acc[:, HEAD:HEAD + 1]).astype(o_ref.dtype)


def _moba_prompt(q, k, v, kmean, pool, page_table, ppb):
    Bn, S, _ = q.shape
    nb = S // MOBA_BLOCK
    assert nb % (MOBA_TRIP_STEPS * MOBA_STEP_BLOCKS) == 0 and MOBA_TRIP_STEPS % 2 == 0 and nb <= HEAD
    Bd, n_pages = page_table.shape
    _, page, H, hd = pool.shape
    n_steps = Bn * N_HEADS * nb
    P = Bd * n_pages // n_steps
    assert P * n_steps == Bd * n_pages and P % ppb == 0 and n_pages % P == 0
    G = n_pages // P

    def page_map(b, h, i, pt, *, p):
        t = (b * N_HEADS + h) * nb + i
        return (pt[t // G, (t % G) * P + p], 0, 0, 0)

    def mean_map(b, h, i, pt):
        t = (b * N_HEADS + h) * nb + i
        return (t // G, t % G, 0, 0)

    s_buf = pltpu.VMEM((MOBA_BLOCK, MOBA_STEP_BLOCKS * MOBA_BLOCK), F32)
    ind = (jnp.arange(S)[:, None] // MOBA_BLOCK == jnp.arange(HEAD)[None, :]).astype(BF16)
    return pl.pallas_call(
        _moba_prompt_kernel,
        grid_spec=pltpu.PrefetchScalarGridSpec(
            num_scalar_prefetch=1,
            grid=(Bn, N_HEADS, nb),
            in_specs=[pl.BlockSpec((1, MOBA_BLOCK, HEAD), lambda b, h, i, pt: (b, i, h)),
                      pl.BlockSpec((1, S, HEAD), lambda b, h, i, pt: (b, 0, h)),
                      pl.BlockSpec((1, S, HEAD), lambda b, h, i, pt: (b, 0, h)),
                      pl.BlockSpec((1, nb, HEAD), lambda b, h, i, pt: (b, 0, h)),
                      pl.BlockSpec((S, HEAD), lambda b, h, i, pt: (0, 0))]
                     + [pl.BlockSpec((1, page, H, hd), functools.partial(page_map, p=p)) for p in range(P)],
            out_specs=[pl.BlockSpec((1, MOBA_BLOCK, HEAD), lambda b, h, i, pt: (b, i, h)),
                       pl.BlockSpec((1, P // ppb, H, hd), mean_map)],
            scratch_shapes=[s_buf, s_buf]),
        out_shape=[jax.ShapeDtypeStruct(q.shape, BF16), jax.ShapeDtypeStruct((Bd, n_pages // ppb, H, hd), F32)],
        compiler_params=_cparams("arbitrary", "arbitrary", "arbitrary"),
        name="moba_prompt",
    )(page_table, q, k, v, kmean, ind, *([pool] * P))


HGRN_HEAD_GROUP = 8


def _hgrn_chunk(q_raw, z, v, gate, lb, gain, st, *, blk, valid):
    C = z.shape[0]
    logf = jnp.log(lb + (1.0 - lb) * _sigmoid(z))
    kk = (1.0 - lb) * _sigmoid(-z)
    rowi = lax.broadcasted_iota(jnp.int32, (C, 1), 0)
    if valid < C:
        logf = jnp.where(rowi < valid, logf, 0.0)
        kk = jnp.where(rowi < valid, kk, 0.0)
    q = _silu(q_raw)
    v = v.astype(BF16)

    tri = lax.broadcasted_iota(jnp.int32, (C, C), 0) >= lax.broadcasted_iota(jnp.int32, (C, C), 1)
    tri_bf = jnp.where(tri, 1.0, 0.0).astype(BF16)
    g1 = logf.astype(BF16)
    r1 = logf - g1.astype(F32)
    g2 = r1.astype(BF16)
    g3 = (r1 - g2.astype(F32)).astype(BF16)
    A = _dot(tri_bf, g1) + _dot(tri_bf, g2) + _dot(tri_bf, g3)
    a_last = A[C - 1:C, :]

    o_inter = _dot_nt((q * jnp.exp(A)).astype(BF16), st.astype(BF16))
    k_dec = (kk * jnp.exp(a_last - A)).astype(BF16)
    st_new = st * jnp.exp(a_last) + _dot_tn(v, k_dec)

    pieces = []
    for j in range(C // blk):
        lo, hi = j * blk, (j + 1) * blk
        a_ref_row = A[lo + blk // 2 - 1:lo + blk // 2, :]
        qt = (q[lo:hi] * jnp.exp(A[lo:hi] - a_ref_row)).astype(BF16)
        kt = (kk * jnp.exp(jnp.where(rowi < hi, a_ref_row - A, NEG_INF))).astype(BF16)
        pieces.append(_dot_nt(qt, kt))
    scores = pieces[0] if len(pieces) == 1 else jnp.concatenate(pieces, axis=0)
    scores = jnp.where(tri, scores, 0.0).astype(BF16)
    o = _dot(scores, v) + o_inter

    y = o * lax.rsqrt(jnp.mean(o * o, axis=-1, keepdims=True) + EPS) * gain
    return y * _silu(gate), st_new


def _hgrn_kernel(q_ref, z_ref, v_ref, g_ref, lb_ref, gn_ref, s0_ref, o_ref, s_ref, st_scr, *, blk):
    n_heads = s0_ref.shape[1]

    @pl.when(pl.program_id(2) == 0)
    def _():
        for h in range(n_heads):
            st_scr[h] = s0_ref[0, h].T

    for h in range(n_heads):
        cols = slice(h * HEAD, (h + 1) * HEAD)
        o, st_new = _hgrn_chunk(q_ref[0, :, cols], z_ref[0, :, cols], v_ref[0, :, cols], g_ref[0, :, cols],
                                lb_ref[:, cols], gn_ref[...], st_scr[h], blk=blk, valid=q_ref.shape[1])
        st_scr[h] = st_new
        s_ref[0, h] = st_new.T
        o_ref[0, :, cols] = o.astype(o_ref.dtype)


def _hgrn_decode_kernel(hg_ref, lb_ref, gn_ref, s0_ref, o_ref, s_ref, *, valid):
    D = N_HEADS * HEAD
    for h in range(N_HEADS):
        cols = [hg_ref[0, :, g * D + h * HEAD:g * D + (h + 1) * HEAD] for g in range(4)]
        o, st_new = _hgrn_chunk(*cols, lb_ref[:, h * HEAD:(h + 1) * HEAD], gn_ref[...], s0_ref[0, h].T,
                                blk=hg_ref.shape[1], valid=valid)
        s_ref[0, h] = st_new.T
        o_ref[0, :, h * HEAD:(h + 1) * HEAD] = o.astype(o_ref.dtype)


def _hgrn_decode(hg, lb, g_norm, s0, *, valid):
    Bn, R, _ = hg.shape
    D = N_HEADS * HEAD
    state = pl.BlockSpec((1, N_HEADS, HEAD, HEAD), lambda b: (b, 0, 0, 0))
    return pl.pallas_call(
        functools.partial(_hgrn_decode_kernel, valid=valid),
        grid=(Bn,),
        in_specs=[pl.BlockSpec((1, R, 4 * D), lambda b: (b, 0, 0)), pl.BlockSpec((1, D), lambda b: (0, 0)),
                  pl.BlockSpec((1, HEAD), lambda b: (0, 0)), state],
        out_specs=[pl.BlockSpec((1, R, D), lambda b: (b, 0, 0)), state],
        out_shape=[jax.ShapeDtypeStruct((Bn, R, D), BF16), jax.ShapeDtypeStruct(s0.shape, F32)],
        compiler_params=_cparams("parallel"),
        name="hgrn_decode",
    )(hg, lb.reshape(1, D), g_norm.reshape(1, HEAD), s0)


def _hgrn(hg, lb, g_norm, s0, *, chunk, blk):
    Bn, T, _ = hg.shape
    H = N_HEADS
    HG = HGRN_HEAD_GROUP
    W = HG * HEAD
    col = lambda g: pl.BlockSpec((1, chunk, W), lambda b, h, c: (b, c, g * (H // HG) + h))
    vec = pl.BlockSpec((1, W), lambda b, h, c: (0, h))
    state = pl.BlockSpec((1, HG, HEAD, HEAD), lambda b, h, c: (b, h, 0, 0))
    return pl.pallas_call(
        functools.partial(_hgrn_kernel, blk=blk),
        grid=(Bn, H // HG, T // chunk),
        in_specs=[col(0), col(1), col(2), col(3), vec, pl.BlockSpec((1, HEAD), lambda b, h, c: (0, 0)), state],
        out_specs=[pl.BlockSpec((1, chunk, W), lambda b, h, c: (b, c, h)), state],
        out_shape=[jax.ShapeDtypeStruct((Bn, T, H * HEAD), BF16), jax.ShapeDtypeStruct(s0.shape, F32)],
        scratch_shapes=[pltpu.VMEM((HG, HEAD, HEAD), F32)],
        compiler_params=_cparams("parallel", "parallel", "arbitrary"),
        name="hgrn",
    )(hg, hg, hg, hg, lb.reshape(1, H * HEAD), g_norm.reshape(1, HEAD), s0)


def _merge_kernel(oa_ref, ob_ref, gates_ref, x_ref, wa_ref, wb_ref, wo_ref, gain_ref, x1_ref, h2_ref):
    D = x_ref.shape[1]
    mix = gates_ref[:, :D] * _dot(oa_ref[...], wa_ref[...]) + gates_ref[:, D:] * _dot(ob_ref[...], wb_ref[...])
    x1 = x_ref[...] + _dot(mix.astype(BF16), wo_ref[...])
    x1_ref[...] = x1
    h2_ref[...] = (x1 * lax.rsqrt(jnp.mean(x1 * x1, axis=-1, keepdims=True) + EPS) * gain_ref[...]).astype(h2_ref.dtype)


def _resident(shape):
    return pl.BlockSpec(shape, lambda *_: (0,) * len(shape), pipeline_mode=pl.Buffered(1))


def _merge_out(o_a, o_b, gates, x, w_ba, w_bb, w_out, norm_ffn, *, tm):
    M, D = x.shape
    DA = o_a.shape[1]
    rows = lambda w: pl.BlockSpec((tm, w), lambda i: (i, 0))
    return pl.pallas_call(
        _merge_kernel,
        grid=(M // tm,),
        in_specs=[rows(DA), rows(DA), rows(2 * D), rows(D), _resident((DA, D)), _resident((DA, D)), _resident((D, D)),
                  _resident((1, D))],
        out_specs=[rows(D), rows(D)],
        out_shape=[jax.ShapeDtypeStruct((M, D), F32), jax.ShapeDtypeStruct((M, D), BF16)],
        compiler_params=_cparams("parallel"),
        name="merge_out",
    )(o_a, o_b, gates, x, w_ba, w_bb, w_out, norm_ffn.reshape(1, D))


def _ffn_init(x_ref, e1_ref, e2_ref, y_ref, tail_scr, *, tiles_per_seq, period):
    i, f = pl.program_id(0), pl.program_id(1)

    @pl.when(f == 0)
    def _():
        y_ref[...] = x_ref[...]

    if period is None:
        @pl.when(i % tiles_per_seq == 0)
        def _():
            tail_scr[f, 0:1, :] = e2_ref[0]
            tail_scr[f, 1:2, :] = e1_ref[0]


def _ffn_main(h_ref, wg_ref, wu_ref, wd_ref, cw_ref, cb_ref, e1_ref, e2_ref, y_ref, g_ref, tail_scr, *, period):
    f = pl.program_id(1)
    tm = h_ref.shape[0]
    h = h_ref[...]
    g = _dot(h, wg_ref[...])
    u = _dot(h, wu_ref[...])
    rowi = lax.broadcasted_iota(jnp.int32, (tm, 1), 0)
    r1 = pltpu.roll(g, 1, 0)
    r2 = pltpu.roll(g, 2, 0)
    if period is None:
        p2 = tail_scr[f, 0:1, :]
        p1 = tail_scr[f, 1:2, :]
        g1 = jnp.where(rowi == 0, p1, r1)
        g2 = jnp.where(rowi == 0, p2, jnp.where(rowi == 1, p1, r2))
        tail_scr[f, 0:2, :] = g[tm - 2:tm]
    else:
        t = rowi % period
        g1 = jnp.where(t == 0, e1_ref[...], r1)
        g2 = jnp.where(t < 2, e2_ref[...], r2)
    g_ref[...] = g.reshape(g_ref.shape) if period is not None else g[tm - 8:tm].reshape(g_ref.shape)
    cw = cw_ref[...]
    c = cb_ref[...] + cw[0:1] * g2 + cw[1:2] * g1 + cw[2:3] * g
    y_ref[...] += _dot((_silu(c) * u).astype(BF16), wd_ref[...])


def _ffn_kernel(h_ref, wg_ref, wu_ref, wd_ref, cw_ref, cb_ref, x_ref, e1_ref, e2_ref,
                y_ref, g_ref, tail_scr, *, tiles_per_seq, period):
    _ffn_init(x_ref, e1_ref, e2_ref, y_ref, tail_scr, tiles_per_seq=tiles_per_seq, period=period)
    _ffn_main(h_ref, wg_ref, wu_ref, wd_ref, cw_ref, cb_ref, e1_ref, e2_ref, y_ref, g_ref, tail_scr, period=period)


def _ffn_decode_kernel(pg_ref, h_ref, wg_ref, wu_ref, wd_ref, cw_ref, cb_ref, x_ref, e1_ref, e2_ref,
                       q_ref, kn_ref, vn_ref, pk_ref, pv_ref, y_ref, g_ref, o_ref, tail_scr, kbuf, vbuf, sem,
                       *, tiles_per_seq, n_decode, n_tok, n_slab, scale):
    step = pl.program_id(0) * pl.num_programs(1) + pl.program_id(1)
    _decode_wait(pg_ref, pk_ref, pv_ref, kbuf, vbuf, sem, step, n_decode, n_tok * n_slab)
    _ffn_init(x_ref, e1_ref, e2_ref, y_ref, tail_scr, tiles_per_seq=tiles_per_seq, period=None)
    _decode_start_next(pg_ref, pk_ref, pv_ref, kbuf, vbuf, sem, step, n_decode, n_tok * n_slab)
    slot = jnp.minimum(step, n_decode - 1) % 2
    _decode_tokens(q_ref, kn_ref, vn_ref, kbuf, vbuf, slot, o_ref, n_tok=n_tok, n_slab=n_slab, scale=scale)
    _ffn_main(h_ref, wg_ref, wu_ref, wd_ref, cw_ref, cb_ref, e1_ref, e2_ref, y_ref, g_ref, tail_scr, period=None)


def _ffn_decode(h2, x1, w_gate, w_up, w_down, conv_w, conv_b, e1, e2, q_pad, kn_pad, vn_pad, pool_k, pool_v,
                sel_pages, *, tm, tf, rows_per_seq, n_tok):
    M, D = x1.shape
    F = w_gate.shape[1]
    nf = F // tf
    tps = rows_per_seq // tm
    Bd, R, _ = q_pad.shape
    page = pool_k.shape[1]
    n_decode = Bd * N_HEADS
    n_slab = sel_pages.shape[0] // (n_decode * n_tok)
    assert (M // tm) * nf >= n_decode

    def dec_map(i, f, pg):
        d = jnp.minimum(i * nf + f, n_decode - 1)
        return (d // N_HEADS, 0, d % N_HEADS)

    e_spec = pl.BlockSpec((1, 1, tf), lambda i, f, pg: (i // tps, 0, f))
    row = pl.BlockSpec((1, R, HEAD), dec_map)
    hbm = pl.BlockSpec(memory_space=pl.ANY)
    slabs = pltpu.VMEM((2, n_tok * n_slab, page, HEAD), F32)
    return pl.pallas_call(
        functools.partial(_ffn_decode_kernel, tiles_per_seq=tps, n_decode=n_decode, n_tok=n_tok, n_slab=n_slab,
                          scale=HEAD ** -0.5),
        grid_spec=pltpu.PrefetchScalarGridSpec(
            num_scalar_prefetch=1,
            grid=(M // tm, nf),
            in_specs=[pl.BlockSpec((tm, D), lambda i, f, pg: (i, 0)),
                      pl.BlockSpec((D, tf), lambda i, f, pg: (0, f)),
                      pl.BlockSpec((D, tf), lambda i, f, pg: (0, f)),
                      pl.BlockSpec((tf, D), lambda i, f, pg: (f, 0)),
                      pl.BlockSpec((CONV_W, tf), lambda i, f, pg: (0, f)),
                      pl.BlockSpec((1, tf), lambda i, f, pg: (0, f)),
                      pl.BlockSpec((tm, D), lambda i, f, pg: (i, 0)),
                      e_spec, e_spec, row, row, row, hbm, hbm],
            out_specs=[pl.BlockSpec((tm, D), lambda i, f, pg: (i, 0)),
                       pl.BlockSpec((1, 8, tf), lambda i, f, pg: (i, 0, f)), row],
            scratch_shapes=[pltpu.VMEM((nf, 8, tf), F32), slabs, slabs, pltpu.SemaphoreType.DMA((2, 2))]),
        out_shape=[jax.ShapeDtypeStruct((M, D), F32), jax.ShapeDtypeStruct((M // tm, 8, F), F32),
                   jax.ShapeDtypeStruct(q_pad.shape, F32)],
        compiler_params=_cparams("arbitrary", "arbitrary"),
        name="ffn_decode",
    )(sel_pages, h2, w_gate, w_up, w_down, conv_w, conv_b.reshape(1, F), x1, e1, e2, q_pad, kn_pad, vn_pad,
      pool_k, pool_v)


def _ffn(h2, x1, w_gate, w_up, w_down, conv_w, conv_b, e1, e2, *, tm, tf, rows_per_seq, period):
    M, D = x1.shape
    F = w_gate.shape[1]
    nf = F // tf
    if period is None:
        tps = rows_per_seq // tm
        e_spec = pl.BlockSpec((1, 1, tf), lambda i, f: (i // tps, 0, f))
        g_shape = jax.ShapeDtypeStruct((M // tm, 8, F), F32)
        g_spec = pl.BlockSpec((1, 8, tf), lambda i, f: (i, 0, f))
    else:
        tps = 1
        e_spec = pl.BlockSpec((tm, tf), lambda i, f: (i, f))
        g_shape = jax.ShapeDtypeStruct((M, F), F32)
        g_spec = pl.BlockSpec((tm, tf), lambda i, f: (i, f))
    return pl.pallas_call(
        functools.partial(_ffn_kernel, tiles_per_seq=tps, period=period),
        grid=(M // tm, nf),
        in_specs=[pl.BlockSpec((tm, D), lambda i, f: (i, 0)),
                  pl.BlockSpec((D, tf), lambda i, f: (0, f)),
                  pl.BlockSpec((D, tf), lambda i, f: (0, f)),
                  pl.BlockSpec((tf, D), lambda i, f: (f, 0)),
                  pl.BlockSpec((CONV_W, tf), lambda i, f: (0, f)),
                  pl.BlockSpec((1, tf), lambda i, f: (0, f)),
                  pl.BlockSpec((tm, D), lambda i, f: (i, 0)),
                  e_spec, e_spec],
        out_specs=[pl.BlockSpec((tm, D), lambda i, f: (i, 0)), g_spec],
        out_shape=[jax.ShapeDtypeStruct((M, D), F32), g_shape],
        scratch_shapes=[pltpu.VMEM((nf, 8, tf), F32)],
        compiler_params=_cparams("arbitrary", "arbitrary"),
        name="ffn",
    )(h2, w_gate, w_up, w_down, conv_w, conv_b.reshape(1, F), x1, e1, e2)


def _ple_kernel(x_ref, gain_ref, wg_ref, p_ref, wp_ref, o_ref):
    x = x_ref[...]
    hn = (x * lax.rsqrt(jnp.mean(x * x, axis=-1, keepdims=True) + EPS) * gain_ref[...]).astype(BF16)
    gate = _sigmoid(_dot(hn, wg_ref[...]))
    o_ref[...] = x + gate * _dot(p_ref[...].astype(BF16), wp_ref[...])


def _ple(x2, p, norm_ple, w_gate, w_proj, *, tm):
    M, D = x2.shape
    P = p.shape[1]
    return pl.pallas_call(
        _ple_kernel,
        grid=(M // tm,),
        in_specs=[pl.BlockSpec((tm, D), lambda i: (i, 0)), _resident((1, D)), _resident((D, D)),
                  pl.BlockSpec((tm, P), lambda i: (i, 0)), _resident((P, D))],
        out_specs=pl.BlockSpec((tm, D), lambda i: (i, 0)),
        out_shape=jax.ShapeDtypeStruct((M, D), F32),
        compiler_params=_cparams("parallel"),
        name="ple",
    )(x2, norm_ple.reshape(1, D), w_gate, p, w_proj)


def _decode_select_kernel(q_ref, km_ref, o_ref):
    nb = km_ref.shape[2]
    for h in range(N_HEADS):
        q = q_ref[0, :, h * HEAD:(h + 1) * HEAD].astype(BF16)
        km = km_ref[0, h]
        km_hi = km.astype(BF16)
        km_lo = (km - km_hi.astype(F32)).astype(BF16)
        g = _dot_nt(q, km_hi) + _dot_nt(q, km_lo)
        blk = lax.broadcasted_iota(jnp.int32, g.shape, 1)
        lane = lax.broadcasted_iota(jnp.int32, (g.shape[0], 128), 1)
        out = jnp.zeros((g.shape[0], 128), jnp.int32)
        for k in range(MOBA_TOPK):
            mx = jnp.max(g, axis=-1, keepdims=True)
            idx = jnp.min(jnp.where(g == mx, blk, nb), axis=-1, keepdims=True)
            out = jnp.where(lane == k, idx, out)
            g = jnp.where(blk == idx, -jnp.inf, g)
        o_ref[0, h] = out


def _decode_select(q_pad, kmean):
    Bd, R, _ = q_pad.shape
    nb = kmean.shape[2]
    return pl.pallas_call(
        _decode_select_kernel,
        grid=(Bd,),
        in_specs=[pl.BlockSpec((1, R, N_HEADS * HEAD), lambda s: (s, 0, 0)),
                  pl.BlockSpec((1, N_HEADS, nb, HEAD), lambda s: (s, 0, 0, 0))],
        out_specs=pl.BlockSpec((1, N_HEADS, R, 128), lambda s: (s, 0, 0, 0)),
        out_shape=jax.ShapeDtypeStruct((Bd, N_HEADS, R, 128), jnp.int32),
        compiler_params=_cparams("parallel"),
        name="decode_select",
    )(q_pad, kmean)


def _slab_copies(pg_ref, pk_ref, pv_ref, kbuf, vbuf, sem, d, per_step):
    head = d % N_HEADS
    sl = d % 2
    copies = []
    for j in range(per_step):
        pg = pg_ref[d * per_step + j]
        copies.append(pltpu.make_async_copy(pk_ref.at[pg, :, head, :], kbuf.at[sl, j], sem.at[0, sl]))
        copies.append(pltpu.make_async_copy(pv_ref.at[pg, :, head, :], vbuf.at[sl, j], sem.at[1, sl]))
    return copies


def _decode_wait(pg_ref, pk_ref, pv_ref, kbuf, vbuf, sem, step, n_decode, per_step):
    @pl.when(step == 0)
    def _():
        for c in _slab_copies(pg_ref, pk_ref, pv_ref, kbuf, vbuf, sem, step, per_step):
            c.start()

    @pl.when(step < n_decode)
    def _():
        for c in _slab_copies(pg_ref, pk_ref, pv_ref, kbuf, vbuf, sem, step, per_step):
            c.wait()


def _decode_start_next(pg_ref, pk_ref, pv_ref, kbuf, vbuf, sem, step, n_decode, per_step):
    @pl.when(step + 1 < n_decode)
    def _():
        for c in _slab_copies(pg_ref, pk_ref, pv_ref, kbuf, vbuf, sem, step + 1, per_step):
            c.start()


def _decode_tokens(q_ref, kn_ref, vn_ref, kbuf, vbuf, slot, o_ref, *, n_tok, n_slab, scale):
    R = q_ref.shape[1]
    page = kbuf.shape[2]
    kn = kn_ref[0]
    vn = vn_ref[0]
    jrow = lax.broadcasted_iota(jnp.int32, (R, 1), 0)
    rows = []
    for t in range(n_tok):
        ks = kbuf[slot, t * n_slab:(t + 1) * n_slab].reshape(n_slab * page, HEAD)
        vs = vbuf[slot, t * n_slab:(t + 1) * n_slab].reshape(n_slab * page, HEAD)
        qrow = q_ref[0, t:t + 1, :]
        s_sel = jnp.sum(ks * qrow, axis=-1, keepdims=True) * scale
        s_own = jnp.sum(kn * qrow, axis=-1, keepdims=True) * scale
        s_own = jnp.where(jrow <= t, s_own, NEG_INF)
        m = jnp.maximum(jnp.max(s_sel, axis=0, keepdims=True), jnp.max(s_own, axis=0, keepdims=True))
        p_sel = jnp.exp(s_sel - m)
        p_own = jnp.exp(s_own - m)
        l = jnp.sum(p_sel, axis=0, keepdims=True) + jnp.sum(p_own, axis=0, keepdims=True)
        o = jnp.sum(p_sel * vs, axis=0, keepdims=True) + jnp.sum(p_own * vn, axis=0, keepdims=True)
        rows.append(o / l)
    rows.append(jnp.zeros((R - n_tok, HEAD), F32))
    o_ref[0] = jnp.concatenate(rows, axis=0)


def _pad_rows(a, rows):
    return jnp.pad(a, ((0, 0), (0, rows - a.shape[1]), (0, 0)))


def kernel(x_prompt, x_sample, cache_k, cache_v, state_hgrn, state_conv, page_table, p_prompt, p_sample,
           norm_mix, w_in, q_norm, k_norm, lb_logits, g_norm_b, w_branch_a, w_branch_b, w_out,
           norm_ffn, w_ffn_gate, w_ffn_up, conv_w, conv_b, w_ffn_down, norm_ple, w_ple_gate, w_ple_proj):
    Bp, S, D = x_prompt.shape
    Bd, T, _ = x_sample.shape
    depth, n_pool, page, H, hd = cache_k.shape
    n_pages = page_table.shape[1]
    F = w_ffn_gate.shape[-1]
    DA = H * hd
    ppb = MOBA_BLOCK // page
    assert depth == 1 and H == N_HEADS and hd == HEAD
    assert (n_pages * page) % MOBA_BLOCK == 0, "past tokens must fill whole attention blocks"
    assert n_pages // ppb >= MOBA_TOPK and S % MOBA_BLOCK == 0 and T <= 8

    bf = lambda w: w[0].astype(BF16)
    w_in_b, w_ba, w_bb, w_o = bf(w_in), bf(w_branch_a), bf(w_branch_b), bf(w_out)
    w_fg, w_fu, w_fd, w_pg, w_pp = bf(w_ffn_gate), bf(w_ffn_up), bf(w_ffn_down), bf(w_ple_gate), bf(w_ple_proj)
    lb = jnp.cumsum(jax.nn.softmax(lb_logits.astype(F32), axis=0), axis=0)[0]

    Mp = Bp * S
    tm = 512
    xp = x_prompt.reshape(Mp, D)
    cos_p, sin_p = _rope_tables(jnp.arange(S))
    h = _rmsnorm(xp, norm_mix[0], tm)
    q, (k, k_bf, kmean), v, v_bf, hg, gates = _in_proj(
        h, w_in_b, q_norm[0], k_norm[0], cos_p, sin_p, tm=tm, rope_tiles=S // tm, with_mean=True, q_dtype=BF16,
        q_scale=HEAD ** -0.5 * LOG2E, row_parts=2)
    pool_k = cache_k.reshape(n_pool, page, H, hd)
    o_a, kmean_s = _moba_prompt(q.reshape(Bp, S, DA), k_bf.reshape(Bp, S, DA), v_bf.reshape(Bp, S, DA),
                                kmean.reshape(Bp, S // MOBA_BLOCK, DA), pool_k, page_table, ppb)
    o_b, hgrn_p = _hgrn(hg.reshape(Bp, S, 4 * DA), lb, g_norm_b[0], jnp.zeros((Bp, H, hd, hd), F32),
                        chunk=256, blk=32)
    x1, h2 = _merge_out(o_a.reshape(Mp, DA), o_b.reshape(Mp, DA), gates, xp, w_ba, w_bb, w_o, norm_ffn[0], tm=256)

    Ms = Bd * T
    R = 8
    xs = x_sample.reshape(Ms, D)
    cos_s, sin_s = _rope_tables(n_pages * page + jnp.arange(T))
    cos_s, sin_s = jnp.tile(cos_s, (Bd, 1)), jnp.tile(sin_s, (Bd, 1))
    hs = _rmsnorm(xs, norm_mix[0], Ms)
    qs, (ks, _), vs, _, hgs, gates_s = _in_proj(
        hs, w_in_b, q_norm[0], k_norm[0], cos_s, sin_s, tm=Ms, rope_tiles=1, with_mean=False, q_dtype=F32, q_scale=1.0)
    q_pad = _pad_rows(qs.reshape(Bd, T, DA), R)
    sel = _decode_select(q_pad, jnp.transpose(kmean_s, (0, 2, 1, 3)))[:, :, :T, :MOBA_TOPK]
    blk_pages = page_table.reshape(Bd, n_pages // ppb, ppb)
    sel_pages = blk_pages[jnp.arange(Bd)[:, None, None, None], sel]

    zeros_e = jnp.zeros((Bp, 1, F), F32)
    x2, g_tail, o_as = _ffn_decode(
        h2, x1, w_fg, w_fu, w_fd, conv_w[0], conv_b[0], zeros_e, zeros_e,
        q_pad, _pad_rows(ks.reshape(Bd, T, DA), R), _pad_rows(vs.reshape(Bd, T, DA), R),
        pool_k, cache_v.reshape(n_pool, page, H, hd), sel_pages.reshape(-1).astype(jnp.int32),
        tm=512, tf=512, rows_per_seq=S, n_tok=T)
    o_as = o_as[:, :T]
    y_prompt = _ple(x2, p_prompt[0].reshape(Mp, -1), norm_ple[0], w_pg, w_pp, tm=256).reshape(Bp, S, D)
    k_prompt = k.reshape(1, Bp, S, H, hd)
    v_prompt = v.reshape(1, Bp, S, H, hd)
    tps = S // 512
    conv_prompt = g_tail[tps - 1::tps, 8 - (CONV_W - 1):, :][None]

    RH = 16
    o_bs, hgrn_s = _hgrn_decode(_pad_rows(hgs.reshape(Bd, T, 4 * DA), RH), lb, g_norm_b[0], state_hgrn[0], valid=T)
    x1s, h2s = _merge_out(o_as.reshape(Ms, DA).astype(BF16), o_bs[:, :T].reshape(Ms, DA), gates_s, xs,
                          w_ba, w_bb, w_o, norm_ffn[0], tm=Ms)
    buf = state_conv[0]
    zrow = jnp.zeros((Bd, 1, F), F32)
    e1 = jnp.concatenate([buf[:, 1:2], jnp.tile(zrow, (1, T - 1, 1))], axis=1).reshape(Ms, F)
    e2 = jnp.concatenate([buf[:, 0:1], buf[:, 1:2], jnp.tile(zrow, (1, T - 2, 1))], axis=1).reshape(Ms, F)
    x2s, g_s = _ffn(h2s, x1s, w_fg, w_fu, w_fd, conv_w[0], conv_b[0], e1, e2,
                    tm=Ms, tf=512, rows_per_seq=T, period=T)
    y_sample = _ple(x2s, p_sample[0].reshape(Ms, -1), norm_ple[0], w_pg, w_pp, tm=Ms).reshape(Bd, T, D)
    k_sample = ks.reshape(1, Bd, T, H, hd)
    v_sample = vs.reshape(1, Bd, T, H, hd)
    conv_sample = g_s.reshape(Bd, T, F)[:, T - (CONV_W - 1):][None]

    return (y_prompt, y_sample, k_prompt, v_prompt, hgrn_p[None], conv_prompt,
            k_sample, v_sample, hgrn_s[None], conv_sample)
```

```python
import functools

import jax
import jax.numpy as jnp
from jax import lax
from jax.experimental import pallas as pl
from jax.experimental.pallas import tpu as pltpu

F32 = jnp.float32
BF16 = jnp.bfloat16
EPS = 1e-6
NEG_INF = -1e30
ROPE_THETA = 10000.0
LOG2E = 1.4426950408889634

HEAD = 128
N_HEADS = 8
MOBA_BLOCK = 256
MOBA_TOPK = 3
CONV_W = 3

VMEM_LIMIT = 52 * 1024 * 1024


def _cparams(*sem):
    return pltpu.CompilerParams(dimension_semantics=sem, vmem_limit_bytes=VMEM_LIMIT)


def _dot(a, b):
    return jnp.dot(a, b, preferred_element_type=F32)


def _dot_nt(a, b):
    return lax.dot_general(a, b, (((1,), (1,)), ((), ())), preferred_element_type=F32)


def _dot_tn(a, b):
    return lax.dot_general(a, b, (((0,), (0,)), ((), ())), preferred_element_type=F32)


def _sigmoid(x):
    return 1.0 / (1.0 + jnp.exp(-x))


def _silu(x):
    return x * _sigmoid(x)


def _rmsnorm_kernel(x_ref, g_ref, o_ref):
    x = x_ref[...]
    ms = jnp.mean(x * x, axis=-1, keepdims=True)
    o_ref[...] = (x * lax.rsqrt(ms + EPS) * g_ref[...]).astype(o_ref.dtype)


def _rmsnorm(x, gain, tm):
    M, D = x.shape
    return pl.pallas_call(
        _rmsnorm_kernel,
        grid=(M // tm,),
        in_specs=[pl.BlockSpec((tm, D), lambda i: (i, 0)), pl.BlockSpec((1, D), lambda i: (0, 0))],
        out_specs=pl.BlockSpec((tm, D), lambda i: (i, 0)),
        out_shape=jax.ShapeDtypeStruct((M, D), BF16),
        compiler_params=_cparams("parallel"),
        name="rmsnorm",
    )(x, gain.reshape(1, D))


def _mm_kernel(*refs, n_extra, epilogue, row_parts):
    a_ref, w_ref = refs[:2]
    extras = refs[2:2 + n_extra]
    outs = refs[2 + n_extra:]
    tm = a_ref.shape[0]
    for part in range(row_parts):
        def rows_of(n_rows):
            return slice(part * (n_rows // row_parts), (part + 1) * (n_rows // row_parts))
        acc = _dot(a_ref[rows_of(tm), :], w_ref[...])
        res = epilogue(acc, *[e[rows_of(tm), :] if e.shape[0] == tm else e[...] for e in extras])
        for o, r in zip(outs, res):
            if len(o.shape) == 2:
                o[rows_of(tm), :] = r.astype(o.dtype)
            else:
                o[0, rows_of(o.shape[1]), :] = r.astype(o.dtype)


def _matmul(a, w, *, col0, n_cols, tm, tn, epilogue, extras=(), extra_specs=(), out_shapes, out_specs, name,
            row_parts=1):
    M, K = a.shape
    cb = col0 // tn
    return pl.pallas_call(
        functools.partial(_mm_kernel, n_extra=len(extras), epilogue=epilogue, row_parts=row_parts),
        grid=(n_cols // tn, M // tm),
        in_specs=[pl.BlockSpec((tm, K), lambda j, i: (i, 0)),
                  pl.BlockSpec((K, tn), lambda j, i: (0, cb + j))] + list(extra_specs),
        out_specs=out_specs,
        out_shape=out_shapes,
        compiler_params=_cparams("parallel", "parallel"),
        name=name,
    )(a, w, *extras)


def _qk_norm_rope(acc, gain, cos, sin):
    outs = []
    for h in range(acc.shape[1] // HEAD):
        z = acc[:, h * HEAD:(h + 1) * HEAD]
        y = z * lax.rsqrt(jnp.mean(z * z, axis=-1, keepdims=True) + EPS) * gain
        outs.append(y * cos + pltpu.roll(y, HEAD // 2, 1) * sin)
    return jnp.concatenate(outs, axis=1)


def _q_epilogue(acc, gain, cos, sin, *, q_scale):
    return [_qk_norm_rope(acc, gain, cos, sin) * q_scale]


def _k_epilogue(acc, gain, cos, sin, *, with_mean):
    k = _qk_norm_rope(acc, gain, cos, sin)
    if not with_mean:
        return [k, k]
    nblk = k.shape[0] // MOBA_BLOCK
    means = [jnp.mean(k[n * MOBA_BLOCK:(n + 1) * MOBA_BLOCK], axis=0, keepdims=True) for n in range(nblk)]
    return [k, k, jnp.concatenate(means, axis=0)]


def _rope_tables(pos):
    half = HEAD // 2
    inv_freq = jnp.power(ROPE_THETA, -jnp.arange(half, dtype=F32) * (2.0 / HEAD))
    ang = pos.astype(F32)[:, None] * inv_freq[None, :]
    c, s = jnp.cos(ang), jnp.sin(ang)
    return jnp.concatenate([c, c], axis=1), jnp.concatenate([-s, s], axis=1)


def _in_proj(h, w_in, q_gain, k_gain, cos, sin, *, tm, rope_tiles, with_mean, q_dtype, q_scale, row_parts=1):
    M = h.shape[0]
    DA = N_HEADS * HEAD
    row_tile = lambda j, i: (i, j)
    rope_spec = pl.BlockSpec((tm, HEAD), lambda j, i: (i % rope_tiles, 0))
    gain_spec = pl.BlockSpec((1, HEAD), lambda j, i: (0, 0))
    tile = pl.BlockSpec((tm, DA), row_tile)
    (q,) = _matmul(h, w_in, col0=0, n_cols=DA, tm=tm, tn=DA,
                   epilogue=functools.partial(_q_epilogue, q_scale=q_scale),
                   extras=(q_gain.reshape(1, HEAD), cos, sin), extra_specs=(gain_spec, rope_spec, rope_spec),
                   out_shapes=[jax.ShapeDtypeStruct((M, DA), q_dtype)], out_specs=[tile], name="proj_q",
                   row_parts=row_parts)
    k_shapes = [jax.ShapeDtypeStruct((M, DA), F32), jax.ShapeDtypeStruct((M, DA), BF16)]
    k_specs = [tile, tile]
    if with_mean:
        nb = tm // MOBA_BLOCK
        k_shapes.append(jax.ShapeDtypeStruct((M // tm, nb, DA), F32))
        k_specs.append(pl.BlockSpec((1, nb, DA), lambda j, i: (i, 0, j)))
    k_out = _matmul(h, w_in, col0=DA, n_cols=DA, tm=tm, tn=DA,
                    epilogue=functools.partial(_k_epilogue, with_mean=with_mean),
                    extras=(k_gain.reshape(1, HEAD), cos, sin), extra_specs=(gain_spec, rope_spec, rope_spec),
                    out_shapes=k_shapes, out_specs=k_specs, name="proj_k", row_parts=row_parts)
    v, v_bf = _matmul(h, w_in, col0=2 * DA, n_cols=DA, tm=tm, tn=DA, epilogue=lambda acc: [acc, acc],
                      out_shapes=[jax.ShapeDtypeStruct((M, DA), F32), jax.ShapeDtypeStruct((M, DA), BF16)],
                      out_specs=[tile, tile], name="proj_v")
    (hg,) = _matmul(h, w_in, col0=3 * DA, n_cols=4 * DA, tm=tm, tn=DA, epilogue=lambda acc: [acc],
                    out_shapes=[jax.ShapeDtypeStruct((M, 4 * DA), F32)], out_specs=[tile], name="proj_hgrn")
    (gates,) = _matmul(h, w_in, col0=7 * DA, n_cols=4 * DA, tm=tm, tn=DA, epilogue=lambda acc: [_sigmoid(acc)],
                       out_shapes=[jax.ShapeDtypeStruct((M, 4 * DA), F32)], out_specs=[tile], name="proj_gates",
                       row_parts=row_parts)
    return q, k_out, v, v_bf, hg, gates


MOBA_STEP_BLOCKS = 2
MOBA_TRIP_STEPS = 4


def _page_block_means(pages, o_ref):
    ppb = len(pages) // o_ref.shape[1]
    rows = pages[0].shape[1]
    for b in range(o_ref.shape[1]):
        s = jnp.sum(pages[b * ppb][0], axis=0)
        for p in range(1, ppb):
            s = s + jnp.sum(pages[b * ppb + p][0], axis=0)
        o_ref[0, b] = s * (1.0 / (ppb * rows))


def _extend_q(q, km, first_future):
    B = q.shape[0]
    nb = km.shape[0]
    km_hi = km.astype(BF16)
    km_lo = (km - km_hi.astype(F32)).astype(BF16)
    g = _dot_nt(km_hi, q) + _dot_nt(km_lo, q)
    blk = lax.broadcasted_iota(jnp.int32, (nb, B), 0)
    g = jnp.where(blk < first_future, g, NEG_INF)
    sel = jnp.zeros((nb, B), F32)
    for _ in range(MOBA_TOPK):
        mx = jnp.max(g, axis=0, keepdims=True)
        idx = jnp.min(jnp.where(g == mx, blk, nb), axis=0, keepdims=True)
        hit = blk == idx
        sel = jnp.where(hit, 1.0, sel)
        g = jnp.where(hit, -jnp.inf, g)
    sel = jnp.where(blk < first_future, sel, 0.0).astype(BF16)
    eye = (lax.broadcasted_iota(jnp.int32, (nb, HEAD), 0) == lax.broadcasted_iota(jnp.int32, (nb, HEAD), 1))
    sel_rows = _dot_tn(sel, jnp.where(eye, 1.0, 0.0).astype(BF16))
    return jnp.concatenate([q, jnp.where(sel_rows > 0.5, 0.0, NEG_INF).astype(BF16)], axis=1)


def _moba_prompt_kernel(pt_ref, q_ref, k_ref, v_ref, km_ref, ind_ref, *rest):
    *pages, o_ref, pool_mean_ref, s_a, s_b = rest
    cur = pl.program_id(2)
    B = MOBA_BLOCK
    nb = km_ref.shape[1]
    _page_block_means(pages, pool_mean_ref)
    q = q_ref[0]
    lane = lax.broadcasted_iota(jnp.int32, (1, HEAD), 1)
    ones_col = jnp.where(lane == 0, 1.0, 0.0).astype(BF16)

    def with_ones(v):
        return jnp.concatenate([v, jnp.broadcast_to(ones_col, v.shape)], axis=1)

    own = pl.multiple_of(cur * B, B)
    s = _dot_nt(q, k_ref[0, pl.ds(own, B), :])
    row = lax.broadcasted_iota(jnp.int32, (B, B), 0)
    col = lax.broadcasted_iota(jnp.int32, (B, B), 1)
    s = jnp.where(col <= row, s, NEG_INF)
    m = jnp.max(s, axis=-1, keepdims=True)
    acc = _dot(jnp.exp2(s - m).astype(BF16), with_ones(v_ref[0, pl.ds(own, B), :]))

    q_aug = _extend_q(q, km_ref[0], cur)

    W = MOBA_STEP_BLOCKS * B
    T = MOBA_TRIP_STEPS
    last_step = nb // MOBA_STEP_BLOCKS - 1
    n_trips = (cur + T * MOBA_STEP_BLOCKS - 1) // (T * MOBA_STEP_BLOCKS)

    def scores(n):
        start = pl.multiple_of(n * W, W)
        k_aug = jnp.concatenate([k_ref[0, pl.ds(start, W), :], ind_ref[pl.ds(start, W), :]], axis=1)
        return _dot_nt(q_aug, k_aug)

    def softmax_step(s_ref, n, m, acc):
        s = s_ref[...]
        m_new = jnp.maximum(m, jnp.max(s, axis=-1, keepdims=True))
        p = jnp.exp2(s - m_new).astype(BF16)
        start = pl.multiple_of(n * W, W)
        return m_new, jnp.exp2(m - m_new) * acc + _dot(p, with_ones(v_ref[0, pl.ds(start, W), :]))

    def body(j, carry):
        m, acc = carry
        bufs = (s_a, s_b)
        for t in range(T):
            bufs[(t + 1) % 2][...] = scores(jnp.minimum(T * j + t + 1, last_step))
            m, acc = softmax_step(bufs[t % 2], T * j + t, m, acc)
        return m, acc

    s_a[...] = scores(0)
    m, acc = lax.fori_loop(0, n_trips, body, (m, acc))
    o_ref[0] = (acc[:, :HEAD] / acc[:, HEAD:HEAD + 1]).astype(o_ref.dtype)


def _moba_prompt(q, k, v, kmean, pool, page_table, ppb):
    Bn, S, _ = q.shape
    nb = S // MOBA_BLOCK
    assert nb % (MOBA_TRIP_STEPS * MOBA_STEP_BLOCKS) == 0 and MOBA_TRIP_STEPS % 2 == 0 and nb <= HEAD
    Bd, n_pages = page_table.shape
    _, page, H, hd = pool.shape
    n_steps = Bn * N_HEADS * nb
    P = Bd * n_pages // n_steps
    assert P * n_steps == Bd * n_pages and P % ppb == 0 and n_pages % P == 0
    G = n_pages // P

    def page_map(b, h, i, pt, *, p):
        t = (b * N_HEADS + h) * nb + i
        return (pt[t // G, (t % G) * P + p], 0, 0, 0)

    def mean_map(b, h, i, pt):
        t = (b * N_HEADS + h) * nb + i
        return (t // G, t % G, 0, 0)

    s_buf = pltpu.VMEM((MOBA_BLOCK, MOBA_STEP_BLOCKS * MOBA_BLOCK), F32)
    ind = (jnp.arange(S)[:, None] // MOBA_BLOCK == jnp.arange(HEAD)[None, :]).astype(BF16)
    return pl.pallas_call(
        _moba_prompt_kernel,
        grid_spec=pltpu.PrefetchScalarGridSpec(
            num_scalar_prefetch=1,
            grid=(Bn, N_HEADS, nb),
            in_specs=[pl.BlockSpec((1, MOBA_BLOCK, HEAD), lambda b, h, i, pt: (b, i, h)),
                      pl.BlockSpec((1, S, HEAD), lambda b, h, i, pt: (b, 0, h)),
                      pl.BlockSpec((1, S, HEAD), lambda b, h, i, pt: (b, 0, h)),
                      pl.BlockSpec((1, nb, HEAD), lambda b, h, i, pt: (b, 0, h)),
                      pl.BlockSpec((S, HEAD), lambda b, h, i, pt: (0, 0))]
                     + [pl.BlockSpec((1, page, H, hd), functools.partial(page_map, p=p)) for p in range(P)],
            out_specs=[pl.BlockSpec((1, MOBA_BLOCK, HEAD), lambda b, h, i, pt: (b, i, h)),
                       pl.BlockSpec((1, P // ppb, H, hd), mean_map)],
            scratch_shapes=[s_buf, s_buf]),
        out_shape=[jax.ShapeDtypeStruct(q.shape, BF16), jax.ShapeDtypeStruct((Bd, n_pages // ppb, H, hd), F32)],
        compiler_params=_cparams("arbitrary", "arbitrary", "arbitrary"),
        name="moba_prompt",
    )(page_table, q, k, v, kmean, ind, *([pool] * P))


HGRN_HEAD_GROUP = 8


def _hgrn_chunk(q_raw, z, v, gate, lb, gain, st, *, blk, valid):
    C = z.shape[0]
    logf = jnp.log(lb + (1.0 - lb) * _sigmoid(z))
    kk = (1.0 - lb) * _sigmoid(-z)
    rowi = lax.broadcasted_iota(jnp.int32, (C, 1), 0)
    if valid < C:
        logf = jnp.where(rowi < valid, logf, 0.0)
        kk = jnp.where(rowi < valid, kk, 0.0)
    q = _silu(q_raw)
    v = v.astype(BF16)

    tri = lax.broadcasted_iota(jnp.int32, (C, C), 0) >= lax.broadcasted_iota(jnp.int32, (C, C), 1)
    tri_bf = jnp.where(tri, 1.0, 0.0).astype(BF16)
    g1 = logf.astype(BF16)
    r1 = logf - g1.astype(F32)
    g2 = r1.astype(BF16)
    g3 = (r1 - g2.astype(F32)).astype(BF16)
    A = _dot(tri_bf, g1) + _dot(tri_bf, g2) + _dot(tri_bf, g3)
    a_last = A[C - 1:C, :]

    o_inter = _dot_nt((q * jnp.exp(A)).astype(BF16), st.astype(BF16))
    k_dec = (kk * jnp.exp(a_last - A)).astype(BF16)
    st_new = st * jnp.exp(a_last) + _dot_tn(v, k_dec)

    pieces = []
    for j in range(C // blk):
        lo, hi = j * blk, (j + 1) * blk
        a_ref_row = A[lo + blk // 2 - 1:lo + blk // 2, :]
        qt = (q[lo:hi] * jnp.exp(A[lo:hi] - a_ref_row)).astype(BF16)
        kt = (kk * jnp.exp(jnp.where(rowi < hi, a_ref_row - A, NEG_INF))).astype(BF16)
        pieces.append(_dot_nt(qt, kt))
    scores = pieces[0] if len(pieces) == 1 else jnp.concatenate(pieces, axis=0)
    scores = jnp.where(tri, scores, 0.0).astype(BF16)
    o = _dot(scores, v) + o_inter

    y = o * lax.rsqrt(jnp.mean(o * o, axis=-1, keepdims=True) + EPS) * gain
    return y * _silu(gate), st_new


def _hgrn_kernel(q_ref, z_ref, v_ref, g_ref, lb_ref, gn_ref, s0_ref, o_ref, s_ref, st_scr, *, blk):
    n_heads = s0_ref.shape[1]

    @pl.when(pl.program_id(2) == 0)
    def _():
        for h in range(n_heads):
            st_scr[h] = s0_ref[0, h].T

    for h in range(n_heads):
        cols = slice(h * HEAD, (h + 1) * HEAD)
        o, st_new = _hgrn_chunk(q_ref[0, :, cols], z_ref[0, :, cols], v_ref[0, :, cols], g_ref[0, :, cols],
                                lb_ref[:, cols], gn_ref[...], st_scr[h], blk=blk, valid=q_ref.shape[1])
        st_scr[h] = st_new
        s_ref[0, h] = st_new.T
        o_ref[0, :, cols] = o.astype(o_ref.dtype)


def _hgrn_decode_kernel(hg_ref, lb_ref, gn_ref, s0_ref, o_ref, s_ref, *, valid):
    D = N_HEADS * HEAD
    for h in range(N_HEADS):
        cols = [hg_ref[0, :, g * D + h * HEAD:g * D + (h + 1) * HEAD] for g in range(4)]
        o, st_new = _hgrn_chunk(*cols, lb_ref[:, h * HEAD:(h + 1) * HEAD], gn_ref[...], s0_ref[0, h].T,
                                blk=hg_ref.shape[1], valid=valid)
        s_ref[0, h] = st_new.T
        o_ref[0, :, h * HEAD:(h + 1) * HEAD] = o.astype(o_ref.dtype)


def _hgrn_decode(hg, lb, g_norm, s0, *, valid):
    Bn, R, _ = hg.shape
    D = N_HEADS * HEAD
    state = pl.BlockSpec((1, N_HEADS, HEAD, HEAD), lambda b: (b, 0, 0, 0))
    return pl.pallas_call(
        functools.partial(_hgrn_decode_kernel, valid=valid),
        grid=(Bn,),
        in_specs=[pl.BlockSpec((1, R, 4 * D), lambda b: (b, 0, 0)), pl.BlockSpec((1, D), lambda b: (0, 0)),
                  pl.BlockSpec((1, HEAD), lambda b: (0, 0)), state],
        out_specs=[pl.BlockSpec((1, R, D), lambda b: (b, 0, 0)), state],
        out_shape=[jax.ShapeDtypeStruct((Bn, R, D), BF16), jax.ShapeDtypeStruct(s0.shape, F32)],
        compiler_params=_cparams("parallel"),
        name="hgrn_decode",
    )(hg, lb.reshape(1, D), g_norm.reshape(1, HEAD), s0)


def _hgrn(hg, lb, g_norm, s0, *, chunk, blk):
    Bn, T, _ = hg.shape
    H = N_HEADS
    HG = HGRN_HEAD_GROUP
    W = HG * HEAD
    col = lambda g: pl.BlockSpec((1, chunk, W), lambda b, h, c: (b, c, g * (H // HG) + h))
    vec = pl.BlockSpec((1, W), lambda b, h, c: (0, h))
    state = pl.BlockSpec((1, HG, HEAD, HEAD), lambda b, h, c: (b, h, 0, 0))
    return pl.pallas_call(
        functools.partial(_hgrn_kernel, blk=blk),
        grid=(Bn, H // HG, T // chunk),
        in_specs=[col(0), col(1), col(2), col(3), vec, pl.BlockSpec((1, HEAD), lambda b, h, c: (0, 0)), state],
        out_specs=[pl.BlockSpec((1, chunk, W), lambda b, h, c: (b, c, h)), state],
        out_shape=[jax.ShapeDtypeStruct((Bn, T, H * HEAD), BF16), jax.ShapeDtypeStruct(s0.shape, F32)],
        scratch_shapes=[pltpu.VMEM((HG, HEAD, HEAD), F32)],
        compiler_params=_cparams("parallel", "parallel", "arbitrary"),
        name="hgrn",
    )(hg, hg, hg, hg, lb.reshape(1, H * HEAD), g_norm.reshape(1, HEAD), s0)


def _merge_kernel(oa_ref, ob_ref, gates_ref, x_ref, wa_ref, wb_ref, wo_ref, gain_ref, x1_ref, h2_ref):
    D = x_ref.shape[1]
    mix = gates_ref[:, :D] * _dot(oa_ref[...], wa_ref[...]) + gates_ref[:, D:] * _dot(ob_ref[...], wb_ref[...])
    x1 = x_ref[...] + _dot(mix.astype(BF16), wo_ref[...])
    x1_ref[...] = x1
    h2_ref[...] = (x1 * lax.rsqrt(jnp.mean(x1 * x1, axis=-1, keepdims=True) + EPS) * gain_ref[...]).astype(h2_ref.dtype)


def _resident(shape):
    return pl.BlockSpec(shape, lambda *_: (0,) * len(shape), pipeline_mode=pl.Buffered(1))


def _merge_out(o_a, o_b, gates, x, w_ba, w_bb, w_out, norm_ffn, *, tm):
    M, D = x.shape
    DA = o_a.shape[1]
    rows = lambda w: pl.BlockSpec((tm, w), lambda i: (i, 0))
    return pl.pallas_call(
        _merge_kernel,
        grid=(M // tm,),
        in_specs=[rows(DA), rows(DA), rows(2 * D), rows(D), _resident((DA, D)), _resident((DA, D)), _resident((D, D)),
                  _resident((1, D))],
        out_specs=[rows(D), rows(D)],
        out_shape=[jax.ShapeDtypeStruct((M, D), F32), jax.ShapeDtypeStruct((M, D), BF16)],
        compiler_params=_cparams("parallel"),
        name="merge_out",
    )(o_a, o_b, gates, x, w_ba, w_bb, w_out, norm_ffn.reshape(1, D))


def _ffn_init(x_ref, e1_ref, e2_ref, y_ref, tail_scr, *, tiles_per_seq, period):
    i, f = pl.program_id(0), pl.program_id(1)

    @pl.when(f == 0)
    def _():
        y_ref[...] = x_ref[...]

    if period is None:
        @pl.when(i % tiles_per_seq == 0)
        def _():
            tail_scr[f, 0:1, :] = e2_ref[0]
            tail_scr[f, 1:2, :] = e1_ref[0]


def _ffn_main(h_ref, wg_ref, wu_ref, wd_ref, cw_ref, cb_ref, e1_ref, e2_ref, y_ref, g_ref, tail_scr, *, period):
    f = pl.program_id(1)
    tm = h_ref.shape[0]
    h = h_ref[...]
    g = _dot(h, wg_ref[...])
    u = _dot(h, wu_ref[...])
    rowi = lax.broadcasted_iota(jnp.int32, (tm, 1), 0)
    r1 = pltpu.roll(g, 1, 0)
    r2 = pltpu.roll(g, 2, 0)
    if period is None:
        p2 = tail_scr[f, 0:1, :]
        p1 = tail_scr[f, 1:2, :]
        g1 = jnp.where(rowi == 0, p1, r1)
        g2 = jnp.where(rowi == 0, p2, jnp.where(rowi == 1, p1, r2))
        tail_scr[f, 0:2, :] = g[tm - 2:tm]
    else:
        t = rowi % period
        g1 = jnp.where(t == 0, e1_ref[...], r1)
        g2 = jnp.where(t < 2, e2_ref[...], r2)
    g_ref[...] = g.reshape(g_ref.shape) if period is not None else g[tm - 8:tm].reshape(g_ref.shape)
    cw = cw_ref[...]
    c = cb_ref[...] + cw[0:1] * g2 + cw[1:2] * g1 + cw[2:3] * g
    y_ref[...] += _dot((_silu(c) * u).astype(BF16), wd_ref[...])


def _ffn_kernel(h_ref, wg_ref, wu_ref, wd_ref, cw_ref, cb_ref, x_ref, e1_ref, e2_ref,
                y_ref, g_ref, tail_scr, *, tiles_per_seq, period):
    _ffn_init(x_ref, e1_ref, e2_ref, y_ref, tail_scr, tiles_per_seq=tiles_per_seq, period=period)
    _ffn_main(h_ref, wg_ref, wu_ref, wd_ref, cw_ref, cb_ref, e1_ref, e2_ref, y_ref, g_ref, tail_scr, period=period)


def _ffn_decode_kernel(pg_ref, h_ref, wg_ref, wu_ref, wd_ref, cw_ref, cb_ref, x_ref, e1_ref, e2_ref,
                       q_ref, kn_ref, vn_ref, pk_ref, pv_ref, y_ref, g_ref, o_ref, tail_scr, kbuf, vbuf, sem,
                       *, tiles_per_seq, n_decode, n_tok, n_slab, scale):
    step = pl.program_id(0) * pl.num_programs(1) + pl.program_id(1)
    _decode_wait(pg_ref, pk_ref, pv_ref, kbuf, vbuf, sem, step, n_decode, n_tok * n_slab)
    _ffn_init(x_ref, e1_ref, e2_ref, y_ref, tail_scr, tiles_per_seq=tiles_per_seq, period=None)
    _decode_start_next(pg_ref, pk_ref, pv_ref, kbuf, vbuf, sem, step, n_decode, n_tok * n_slab)
    slot = jnp.minimum(step, n_decode - 1) % 2
    _decode_tokens(q_ref, kn_ref, vn_ref, kbuf, vbuf, slot, o_ref, n_tok=n_tok, n_slab=n_slab, scale=scale)
    _ffn_main(h_ref, wg_ref, wu_ref, wd_ref, cw_ref, cb_ref, e1_ref, e2_ref, y_ref, g_ref, tail_scr, period=None)


def _ffn_decode(h2, x1, w_gate, w_up, w_down, conv_w, conv_b, e1, e2, q_pad, kn_pad, vn_pad, pool_k, pool_v,
                sel_pages, *, tm, tf, rows_per_seq, n_tok):
    M, D = x1.shape
    F = w_gate.shape[1]
    nf = F // tf
    tps = rows_per_seq // tm
    Bd, R, _ = q_pad.shape
    page = pool_k.shape[1]
    n_decode = Bd * N_HEADS
    n_slab = sel_pages.shape[0] // (n_decode * n_tok)
    assert (M // tm) * nf >= n_decode

    def dec_map(i, f, pg):
        d = jnp.minimum(i * nf + f, n_decode - 1)
        return (d // N_HEADS, 0, d % N_HEADS)

    e_spec = pl.BlockSpec((1, 1, tf), lambda i, f, pg: (i // tps, 0, f))
    row = pl.BlockSpec((1, R, HEAD), dec_map)
    hbm = pl.BlockSpec(memory_space=pl.ANY)
    slabs = pltpu.VMEM((2, n_tok * n_slab, page, HEAD), F32)
    return pl.pallas_call(
        functools.partial(_ffn_decode_kernel, tiles_per_seq=tps, n_decode=n_decode, n_tok=n_tok, n_slab=n_slab,
                          scale=HEAD ** -0.5),
        grid_spec=pltpu.PrefetchScalarGridSpec(
            num_scalar_prefetch=1,
            grid=(M // tm, nf),
            in_specs=[pl.BlockSpec((tm, D), lambda i, f, pg: (i, 0)),
                      pl.BlockSpec((D, tf), lambda i, f, pg: (0, f)),
                      pl.BlockSpec((D, tf), lambda i, f, pg: (0, f)),
                      pl.BlockSpec((tf, D), lambda i, f, pg: (f, 0)),
                      pl.BlockSpec((CONV_W, tf), lambda i, f, pg: (0, f)),
                      pl.BlockSpec((1, tf), lambda i, f, pg: (0, f)),
                      pl.BlockSpec((tm, D), lambda i, f, pg: (i, 0)),
                      e_spec, e_spec, row, row, row, hbm, hbm],
            out_specs=[pl.BlockSpec((tm, D), lambda i, f, pg: (i, 0)),
                       pl.BlockSpec((1, 8, tf), lambda i, f, pg: (i, 0, f)), row],
            scratch_shapes=[pltpu.VMEM((nf, 8, tf), F32), slabs, slabs, pltpu.SemaphoreType.DMA((2, 2))]),
        out_shape=[jax.ShapeDtypeStruct((M, D), F32), jax.ShapeDtypeStruct((M // tm, 8, F), F32),
                   jax.ShapeDtypeStruct(q_pad.shape, F32)],
        compiler_params=_cparams("arbitrary", "arbitrary"),
        name="ffn_decode",
    )(sel_pages, h2, w_gate, w_up, w_down, conv_w, conv_b.reshape(1, F), x1, e1, e2, q_pad, kn_pad, vn_pad,
      pool_k, pool_v)


def _ffn(h2, x1, w_gate, w_up, w_down, conv_w, conv_b, e1, e2, *, tm, tf, rows_per_seq, period):
    M, D = x1.shape
    F = w_gate.shape[1]
    nf = F // tf
    if period is None:
        tps = rows_per_seq // tm
        e_spec = pl.BlockSpec((1, 1, tf), lambda i, f: (i // tps, 0, f))
        g_shape = jax.ShapeDtypeStruct((M // tm, 8, F), F32)
        g_spec = pl.BlockSpec((1, 8, tf), lambda i, f: (i, 0, f))
    else:
        tps = 1
        e_spec = pl.BlockSpec((tm, tf), lambda i, f: (i, f))
        g_shape = jax.ShapeDtypeStruct((M, F), F32)
        g_spec = pl.BlockSpec((tm, tf), lambda i, f: (i, f))
    return pl.pallas_call(
        functools.partial(_ffn_kernel, tiles_per_seq=tps, period=period),
        grid=(M // tm, nf),
        in_specs=[pl.BlockSpec((tm, D), lambda i, f: (i, 0)),
                  pl.BlockSpec((D, tf), lambda i, f: (0, f)),
                  pl.BlockSpec((D, tf), lambda i, f: (0, f)),
                  pl.BlockSpec((tf, D), lambda i, f: (f, 0)),
                  pl.BlockSpec((CONV_W, tf), lambda i, f: (0, f)),
                  pl.BlockSpec((1, tf), lambda i, f: (0, f)),
                  pl.BlockSpec((tm, D), lambda i, f: (i, 0)),
                  e_spec, e_spec],
        out_specs=[pl.BlockSpec((tm, D), lambda i, f: (i, 0)), g_spec],
        out_shape=[jax.ShapeDtypeStruct((M, D), F32), g_shape],
        scratch_shapes=[pltpu.VMEM((nf, 8, tf), F32)],
        compiler_params=_cparams("arbitrary", "arbitrary"),
        name="ffn",
    )(h2, w_gate, w_up, w_down, conv_w, conv_b.reshape(1, F), x1, e1, e2)


def _ple_kernel(x_ref, gain_ref, wg_ref, p_ref, wp_ref, o_ref):
    x = x_ref[...]
    hn = (x * lax.rsqrt(jnp.mean(x * x, axis=-1, keepdims=True) + EPS) * gain_ref[...]).astype(BF16)
    gate = _sigmoid(_dot(hn, wg_ref[...]))
    o_ref[...] = x + gate * _dot(p_ref[...].astype(BF16), wp_ref[...])


def _ple(x2, p, norm_ple, w_gate, w_proj, *, tm):
    M, D = x2.shape
    P = p.shape[1]
    return pl.pallas_call(
        _ple_kernel,
        grid=(M // tm,),
        in_specs=[pl.BlockSpec((tm, D), lambda i: (i, 0)), _resident((1, D)), _resident((D, D)),
                  pl.BlockSpec((tm, P), lambda i: (i, 0)), _resident((P, D))],
        out_specs=pl.BlockSpec((tm, D), lambda i: (i, 0)),
        out_shape=jax.ShapeDtypeStruct((M, D), F32),
        compiler_params=_cparams("parallel"),
        name="ple",
    )(x2, norm_ple.reshape(1, D), w_gate, p, w_proj)


def _decode_select_kernel(q_ref, km_ref, o_ref):
    nb = km_ref.shape[2]
    for h in range(N_HEADS):
        q = q_ref[0, :, h * HEAD:(h + 1) * HEAD].astype(BF16)
        km = km_ref[0, h]
        km_hi = km.astype(BF16)
        km_lo = (km - km_hi.astype(F32)).astype(BF16)
        g = _dot_nt(q, km_hi) + _dot_nt(q, km_lo)
        blk = lax.broadcasted_iota(jnp.int32, g.shape, 1)
        lane = lax.broadcasted_iota(jnp.int32, (g.shape[0], 128), 1)
        out = jnp.zeros((g.shape[0], 128), jnp.int32)
        for k in range(MOBA_TOPK):
            mx = jnp.max(g, axis=-1, keepdims=True)
            idx = jnp.min(jnp.where(g == mx, blk, nb), axis=-1, keepdims=True)
            out = jnp.where(lane == k, idx, out)
            g = jnp.where(blk == idx, -jnp.inf, g)
        o_ref[0, h] = out


def _decode_select(q_pad, kmean):
    Bd, R, _ = q_pad.shape
    nb = kmean.shape[2]
    return pl.pallas_call(
        _decode_select_kernel,
        grid=(Bd,),
        in_specs=[pl.BlockSpec((1, R, N_HEADS * HEAD), lambda s: (s, 0, 0)),
                  pl.BlockSpec((1, N_HEADS, nb, HEAD), lambda s: (s, 0, 0, 0))],
        out_specs=pl.BlockSpec((1, N_HEADS, R, 128), lambda s: (s, 0, 0, 0)),
        out_shape=jax.ShapeDtypeStruct((Bd, N_HEADS, R, 128), jnp.int32),
        compiler_params=_cparams("parallel"),
        name="decode_select",
    )(q_pad, kmean)


def _slab_copies(pg_ref, pk_ref, pv_ref, kbuf, vbuf, sem, d, per_step):
    head = d % N_HEADS
    sl = d % 2
    copies = []
    for j in range(per_step):
        pg = pg_ref[d * per_step + j]
        copies.append(pltpu.make_async_copy(pk_ref.at[pg, :, head, :], kbuf.at[sl, j], sem.at[0, sl]))
        copies.append(pltpu.make_async_copy(pv_ref.at[pg, :, head, :], vbuf.at[sl, j], sem.at[1, sl]))
    return copies


def _decode_wait(pg_ref, pk_ref, pv_ref, kbuf, vbuf, sem, step, n_decode, per_step):
    @pl.when(step == 0)
    def _():
        for c in _slab_copies(pg_ref, pk_ref, pv_ref, kbuf, vbuf, sem, step, per_step):
            c.start()

    @pl.when(step < n_decode)
    def _():
        for c in _slab_copies(pg_ref, pk_ref, pv_ref, kbuf, vbuf, sem, step, per_step):
            c.wait()


def _decode_start_next(pg_ref, pk_ref, pv_ref, kbuf, vbuf, sem, step, n_decode, per_step):
    @pl.when(step + 1 < n_decode)
    def _():
        for c in _slab_copies(pg_ref, pk_ref, pv_ref, kbuf, vbuf, sem, step + 1, per_step):
            c.start()


def _decode_tokens(q_ref, kn_ref, vn_ref, kbuf, vbuf, slot, o_ref, *, n_tok, n_slab, scale):
    R = q_ref.shape[1]
    page = kbuf.shape[2]
    kn = kn_ref[0]
    vn = vn_ref[0]
    jrow = lax.broadcasted_iota(jnp.int32, (R, 1), 0)
    rows = []
    for t in range(n_tok):
        ks = kbuf[slot, t * n_slab:(t + 1) * n_slab].reshape(n_slab * page, HEAD)
        vs = vbuf[slot, t * n_slab:(t + 1) * n_slab].reshape(n_slab * page, HEAD)
        qrow = q_ref[0, t:t + 1, :]
        s_sel = jnp.sum(ks * qrow, axis=-1, keepdims=True) * scale
        s_own = jnp.sum(kn * qrow, axis=-1, keepdims=True) * scale
        s_own = jnp.where(jrow <= t, s_own, NEG_INF)
        m = jnp.maximum(jnp.max(s_sel, axis=0, keepdims=True), jnp.max(s_own, axis=0, keepdims=True))
        p_sel = jnp.exp(s_sel - m)
        p_own = jnp.exp(s_own - m)
        l = jnp.sum(p_sel, axis=0, keepdims=True) + jnp.sum(p_own, axis=0, keepdims=True)
        o = jnp.sum(p_sel * vs, axis=0, keepdims=True) + jnp.sum(p_own * vn, axis=0, keepdims=True)
        rows.append(o / l)
    rows.append(jnp.zeros((R - n_tok, HEAD), F32))
    o_ref[0] = jnp.concatenate(rows, axis=0)


def _pad_rows(a, rows):
    return jnp.pad(a, ((0, 0), (0, rows - a.shape[1]), (0, 0)))


def kernel(x_prompt, x_sample, cache_k, cache_v, state_hgrn, state_conv, page_table, p_prompt, p_sample,
           norm_mix, w_in, q_norm, k_norm, lb_logits, g_norm_b, w_branch_a, w_branch_b, w_out,
           norm_ffn, w_ffn_gate, w_ffn_up, conv_w, conv_b, w_ffn_down, norm_ple, w_ple_gate, w_ple_proj):
    Bp, S, D = x_prompt.shape
    Bd, T, _ = x_sample.shape
    depth, n_pool, page, H, hd = cache_k.shape
    n_pages = page_table.shape[1]
    F = w_ffn_gate.shape[-1]
    DA = H * hd
    ppb = MOBA_BLOCK // page
    assert depth == 1 and H == N_HEADS and hd == HEAD
    assert (n_pages * page) % MOBA_BLOCK == 0, "past tokens must fill whole attention blocks"
    assert n_pages // ppb >= MOBA_TOPK and S % MOBA_BLOCK == 0 and T <= 8

    bf = lambda w: w[0].astype(BF16)
    w_in_b, w_ba, w_bb, w_o = bf(w_in), bf(w_branch_a), bf(w_branch_b), bf(w_out)
    w_fg, w_fu, w_fd, w_pg, w_pp = bf(w_ffn_gate), bf(w_ffn_up), bf(w_ffn_down), bf(w_ple_gate), bf(w_ple_proj)
    lb = jnp.cumsum(jax.nn.softmax(lb_logits.astype(F32), axis=0), axis=0)[0]

    Mp = Bp * S
    tm = 512
    xp = x_prompt.reshape(Mp, D)
    cos_p, sin_p = _rope_tables(jnp.arange(S))
    h = _rmsnorm(xp, norm_mix[0], tm)
    q, (k, k_bf, kmean), v, v_bf, hg, gates = _in_proj(
        h, w_in_b, q_norm[0], k_norm[0], cos_p, sin_p, tm=1024, rope_tiles=S // 1024, with_mean=True, q_dtype=BF16,
        q_scale=HEAD ** -0.5 * LOG2E, row_parts=4)
    pool_k = cache_k.reshape(n_pool, page, H, hd)
    o_a, kmean_s = _moba_prompt(q.reshape(Bp, S, DA), k_bf.reshape(Bp, S, DA), v_bf.reshape(Bp, S, DA),
                                kmean.reshape(Bp, S // MOBA_BLOCK, DA), pool_k, page_table, ppb)
    o_b, hgrn_p = _hgrn(hg.reshape(Bp, S, 4 * DA), lb, g_norm_b[0], jnp.zeros((Bp, H, hd, hd), F32),
                        chunk=256, blk=32)
    x1, h2 = _merge_out(o_a.reshape(Mp, DA), o_b.reshape(Mp, DA), gates, xp, w_ba, w_bb, w_o, norm_ffn[0], tm=256)

    Ms = Bd * T
    R = 8
    xs = x_sample.reshape(Ms, D)
    cos_s, sin_s = _rope_tables(n_pages * page + jnp.arange(T))
    cos_s, sin_s = jnp.tile(cos_s, (Bd, 1)), jnp.tile(sin_s, (Bd, 1))
    hs = _rmsnorm(xs, norm_mix[0], Ms)
    qs, (ks, _), vs, _, hgs, gates_s = _in_proj(
        hs, w_in_b, q_norm[0], k_norm[0], cos_s, sin_s, tm=Ms, rope_tiles=1, with_mean=False, q_dtype=F32, q_scale=1.0)
    q_pad = _pad_rows(qs.reshape(Bd, T, DA), R)
    sel = _decode_select(q_pad, jnp.transpose(kmean_s, (0, 2, 1, 3)))[:, :, :T, :MOBA_TOPK]
    blk_pages = page_table.reshape(Bd, n_pages // ppb, ppb)
    sel_pages = blk_pages[jnp.arange(Bd)[:, None, None, None], sel]

    zeros_e = jnp.zeros((Bp, 1, F), F32)
    x2, g_tail, o_as = _ffn_decode(
        h2, x1, w_fg, w_fu, w_fd, conv_w[0], conv_b[0], zeros_e, zeros_e,
        q_pad, _pad_rows(ks.reshape(Bd, T, DA), R), _pad_rows(vs.reshape(Bd, T, DA), R),
        pool_k, cache_v.reshape(n_pool, page, H, hd), sel_pages.reshape(-1).astype(jnp.int32),
        tm=512, tf=512, rows_per_seq=S, n_tok=T)
    o_as = o_as[:, :T]
    y_prompt = _ple(x2, p_prompt[0].reshape(Mp, -1), norm_ple[0], w_pg, w_pp, tm=256).reshape(Bp, S, D)
    k_prompt = k.reshape(1, Bp, S, H, hd)
    v_prompt = v.reshape(1, Bp, S, H, hd)
    tps = S // 512
    conv_prompt = g_tail[tps - 1::tps, 8 - (CONV_W - 1):, :][None]

    RH = 16
    o_bs, hgrn_s = _hgrn_decode(_pad_rows(hgs.reshape(Bd, T, 4 * DA), RH), lb, g_norm_b[0], state_hgrn[0], valid=T)
    x1s, h2s = _merge_out(o_as.reshape(Ms, DA).astype(BF16), o_bs[:, :T].reshape(Ms, DA), gates_s, xs,
                          w_ba, w_bb, w_o, norm_ffn[0], tm=Ms)
    buf = state_conv[0]
    zrow = jnp.zeros((Bd, 1, F), F32)
    e1 = jnp.concatenate([buf[:, 1:2], jnp.tile(zrow, (1, T - 1, 1))], axis=1).reshape(Ms, F)
    e2 = jnp.concatenate([buf[:, 0:1], buf[:, 1:2], jnp.tile(zrow, (1, T - 2, 1))], axis=1).reshape(Ms, F)
    x2s, g_s = _ffn(h2s, x1s, w_fg, w_fu, w_fd, conv_w[0], conv_b[0], e1, e2,
                    tm=Ms, tf=512, rows_per_seq=T, period=T)
    y_sample = _ple(x2s, p_sample[0].reshape(Ms, -1), norm_ple[0], w_pg, w_pp, tm=Ms).reshape(Bd, T, D)
    k_sample = ks.reshape(1, Bd, T, H, hd)
    v_sample = vs.reshape(1, Bd, T, H, hd)
    conv_sample = g_s.reshape(Bd, T, F)[:, T - (CONV_W - 1):][None]

    return (y_prompt, y_sample, k_prompt, v_prompt, hgrn_p[None], conv_prompt,
            k_sample, v_sample, hgrn_s[None], conv_sample)
```
